```python
import jax, jax.numpy as jnp
from jax import lax
import numpy as np

D_MODEL = 2048
BATCH = 1
SEQ = 8192
DEPTH = 4

HEAD_DIM = 64
RWKV_HEADS = 12
RWKV_WIDTH = RWKV_HEADS * HEAD_DIM
DECAY_LORA = 64
ICLR_LORA = 64
GATE_LORA = 128
ATTN_Q_HEADS = 12
ATTN_KV_HEADS = 4
ATTN_GROUP = ATTN_Q_HEADS // ATTN_KV_HEADS
ATTN_WIDTH = ATTN_Q_HEADS * HEAD_DIM
WINDOW = 128
GM_HEADS = 4
GM_HEAD_DIM = 128
GM_WIDTH = GM_HEADS * GM_HEAD_DIM
GM_CHUNK = 128
MIX_WIDTH = RWKV_WIDTH + ATTN_WIDTH + GM_WIDTH
D_FF = 4 * D_MODEL
RWKV_COLS = 3 * RWKV_WIDTH + DECAY_LORA + ICLR_LORA + GATE_LORA
ATTN_COLS = ATTN_WIDTH + 2 * ATTN_KV_HEADS * HEAD_DIM
GM_COLS = 2 * GM_WIDTH
IN_COLS = RWKV_COLS + ATTN_COLS + GM_COLS
NORM_EPS = 1e-6
GN_EPS = 64e-5

kernel_name = "hymba_style_rwkv7_swa_gmlp_trunk"


def _rmsnorm(x, g):
    xf = x.astype(jnp.float32)
    y = xf * lax.rsqrt(jnp.mean(xf * xf, axis=-1, keepdims=True) + NORM_EPS)
    return (y * g.astype(jnp.float32)).astype(x.dtype)


def _layernorm(x, g, b):
    xf = x.astype(jnp.float32)
    mu = jnp.mean(xf, axis=-1, keepdims=True)
    var = jnp.mean(jnp.square(xf - mu), axis=-1, keepdims=True)
    y = (xf - mu) * lax.rsqrt(var + NORM_EPS)
    return (y * g.astype(jnp.float32) + b.astype(jnp.float32)).astype(x.dtype)


def _rwkv7_scan(r, decay, k, v, kk, a):
    B, S, H, N = r.shape
    xs = tuple(jnp.moveaxis(t, 1, 0) for t in (r, decay, k, v, -kk, kk * a))

    def step(state, inp):
        r_t, w_t, k_t, v_t, a_t, b_t = inp
        sa = jnp.einsum('bhij,bhj->bhi', state, a_t)
        state = (state * w_t[:, :, None, :] + sa[..., None] * b_t[:, :, None, :]
                 + v_t[..., None] * k_t[:, :, None, :])
        y_t = jnp.einsum('bhij,bhj->bhi', state, r_t)
        return state, y_t

    state0 = jnp.zeros((B, H, N, N), jnp.float32)
    _, y = lax.scan(step, state0, xs)
    return jnp.moveaxis(y, 0, 1)


def _rwkv7_mix(p, mu, w0, decay_up, a0, a_up, g_up, k_k, k_a, r_k, lnx_g, lnx_b):
    B, S, _ = p.shape
    f32 = jnp.float32
    prev = jnp.pad(p, ((0, 0), (1, 0), (0, 0)))[:, :-1]
    p = p + (prev - p) * mu
    cuts = list(np.cumsum([RWKV_WIDTH, RWKV_WIDTH, RWKV_WIDTH, DECAY_LORA, ICLR_LORA]))
    r, k, v, xw, xa, xg = jnp.split(p, cuts, axis=-1)
    w = -jax.nn.softplus(-(w0 + jnp.tanh(xw) @ decay_up)) - 0.5
    decay = jnp.exp(-jnp.exp(w.astype(f32)))
    a = jax.nn.sigmoid(a0 + xa @ a_up)
    g = jax.nn.sigmoid(xg) @ g_up
    heads = lambda t: t.astype(f32).reshape(B, S, RWKV_HEADS, HEAD_DIM)
    kk = heads(k * k_k)
    kk = kk / jnp.maximum(jnp.sqrt(jnp.sum(kk * kk, axis=-1, keepdims=True)), 1e-12)
    k = k * (1.0 + (a - 1.0) * k_a)
    rh, kh, vh = heads(r), heads(k), heads(v)
    y = _rwkv7_scan(rh, heads(decay), kh, vh, kk, heads(a))
    m = jnp.mean(y, axis=-1, keepdims=True)
    var = jnp.mean(jnp.square(y - m), axis=-1, keepdims=True)
    y = ((y - m) * lax.rsqrt(var + GN_EPS)).reshape(B, S, RWKV_WIDTH)
    y = y * lnx_g.astype(f32) + lnx_b.astype(f32)
    bonus = jnp.sum(rh * kh * r_k.astype(f32), axis=-1, keepdims=True) * vh
    y = y + bonus.reshape(B, S, RWKV_WIDTH)
    return (y * g.astype(f32)).astype(p.dtype)


def _swa_sinks(p, sinks):
    B, S, _ = p.shape
    nb = S // WINDOW
    kvw = ATTN_KV_HEADS * HEAD_DIM
    q, k, v = jnp.split(p, [ATTN_WIDTH, ATTN_WIDTH + kvw], axis=-1)
    q = q.reshape(B, nb, WINDOW, ATTN_KV_HEADS, ATTN_GROUP, HEAD_DIM)
    k = k.reshape(B, nb, WINDOW, ATTN_KV_HEADS, HEAD_DIM)
    v = v.reshape(B, nb, WINDOW, ATTN_KV_HEADS, HEAD_DIM)

    def with_prev(t):
        prev = jnp.concatenate([jnp.zeros_like(t[:, :1]), t[:, :-1]], axis=1)
        return jnp.concatenate([prev, t], axis=2)

    kb, vb = with_prev(k), with_prev(v)
    s = jnp.einsum('bnqhgd,bnkhd->bnhgqk', q, kb).astype(jnp.float32) * (HEAD_DIM ** -0.5)
    i = jnp.arange(WINDOW)[:, None]
    j = jnp.arange(2 * WINDOW)[None, :]
    band = (j > i) & (j <= i + WINDOW)
    valid = (jnp.arange(nb)[:, None, None] > 0) | (j >= WINDOW)[None]
    mask = band[None] & valid
    s = jnp.where(mask[None, :, None, None], s, -jnp.inf)
    sink = jnp.broadcast_to(
        sinks.astype(jnp.float32).reshape(1, 1, ATTN_KV_HEADS, ATTN_GROUP, 1, 1),
        s.shape[:-1] + (1,))
    prob = jax.nn.softmax(jnp.concatenate([s, sink], axis=-1), axis=-1)[..., :-1]
    o = jnp.einsum('bnhgqk,bnkhd->bnqhgd', prob.astype(vb.dtype), vb)
    return o.reshape(B, S, ATTN_WIDTH)


def _chunk_gmlp(p, ln_g, ln_b, ws, bs):
    B, S, _ = p.shape
    nc = S // GM_CHUNK
    z = jax.nn.gelu(p, approximate=False)
    u, z2 = jnp.split(z, 2, axis=-1)
    z2 = _layernorm(z2, ln_g, ln_b).reshape(B, nc, GM_CHUNK, GM_HEADS, GM_HEAD_DIM)
    causal = jnp.tril(jnp.ones((GM_CHUNK, GM_CHUNK), dtype=bool))
    w = jnp.where(causal[None], ws, jnp.zeros_like(ws))
    mixed = jnp.einsum('hts,bcshe->bcthe', w, z2) + bs.T[:, :, None]
    u = u.reshape(B, nc, GM_CHUNK, GM_HEADS, GM_HEAD_DIM)
    return (u * mixed).reshape(B, S, GM_WIDTH)


def setup_inputs(seed: int = 0) -> dict:
    key = jax.random.key(seed)
    ks = jax.random.split(key, 32)
    L, D = DEPTH, D_MODEL
    nrm = lambda k, shape, scale: jax.random.normal(k, shape, jnp.float32) * scale
    return {
        "x": nrm(ks[0], (BATCH, SEQ, D), 1.0),
        "ln1_g": 1.0 + nrm(ks[1], (L, D), 0.02),
        "w_in": nrm(ks[2], (L, D, IN_COLS), D ** -0.5),
        "rwkv_mu": jax.random.uniform(ks[3], (L, RWKV_COLS), jnp.float32),
        "rwkv_w0": jax.random.uniform(ks[4], (L, RWKV_WIDTH), jnp.float32, -5.0, 0.5),
        "rwkv_decay_up": nrm(ks[5], (L, DECAY_LORA, RWKV_WIDTH), 0.1 * DECAY_LORA ** -0.5),
        "rwkv_a0": nrm(ks[6], (L, RWKV_WIDTH), 0.1),
        "rwkv_a_up": nrm(ks[7], (L, ICLR_LORA, RWKV_WIDTH), 0.1 * ICLR_LORA ** -0.5),
        "rwkv_g_up": nrm(ks[8], (L, GATE_LORA, RWKV_WIDTH), GATE_LORA ** -0.5),
        "rwkv_k_k": 0.85 + nrm(ks[9], (L, RWKV_WIDTH), 0.05),
        "rwkv_k_a": 1.0 + nrm(ks[10], (L, RWKV_WIDTH), 0.05),
        "rwkv_r_k": nrm(ks[11], (L, RWKV_HEADS, HEAD_DIM), 0.1),
        "rwkv_lnx_g": 1.0 + nrm(ks[12], (L, RWKV_WIDTH), 0.02),
        "rwkv_lnx_b": nrm(ks[13], (L, RWKV_WIDTH), 0.02),
        "attn_sinks": nrm(ks[14], (L, ATTN_Q_HEADS), 1.0),
        "attn_norm_g": 1.0 + nrm(ks[15], (L, ATTN_WIDTH), 0.02),
        "gm_ln_g": 1.0 + nrm(ks[16], (L, GM_WIDTH), 0.02),
        "gm_ln_b": nrm(ks[17], (L, GM_WIDTH), 0.02),
        "gm_ws": nrm(ks[18], (L, GM_HEADS, GM_CHUNK, GM_CHUNK), GM_CHUNK ** -0.5),
        "gm_bs": 1.0 + nrm(ks[19], (L, GM_HEADS, GM_CHUNK), 0.1),
        "gm_norm_g": 1.0 + nrm(ks[20], (L, GM_WIDTH), 0.02),
        "w_out": nrm(ks[21], (L, MIX_WIDTH, D), MIX_WIDTH ** -0.5),
        "ln2_g": 1.0 + nrm(ks[22], (L, D), 0.02),
        "w_ffn_up": nrm(ks[23], (L, D, D_FF), D ** -0.5),
        "w_ffn_down": nrm(ks[24], (L, D_FF, D), D_FF ** -0.5),
        "lnf_g": 1.0 + nrm(ks[25], (D,), 0.02),
    }


def reference(x, ln1_g, w_in, rwkv_mu, rwkv_w0, rwkv_decay_up, rwkv_a0, rwkv_a_up,
              rwkv_g_up, rwkv_k_k, rwkv_k_a, rwkv_r_k, rwkv_lnx_g, rwkv_lnx_b,
              attn_sinks, attn_norm_g, gm_ln_g, gm_ln_b, gm_ws, gm_bs, gm_norm_g,
              w_out, ln2_g, w_ffn_up, w_ffn_down, lnf_g):
    for l in range(DEPTH):
        h = _rmsnorm(x, ln1_g[l])
        p = h @ w_in[l]
        p_r, p_a, p_g = jnp.split(p, [RWKV_COLS, RWKV_COLS + ATTN_COLS], axis=-1)
        y_r = _rwkv7_mix(p_r, rwkv_mu[l], rwkv_w0[l], rwkv_decay_up[l], rwkv_a0[l],
                         rwkv_a_up[l], rwkv_g_up[l], rwkv_k_k[l], rwkv_k_a[l],
                         rwkv_r_k[l], rwkv_lnx_g[l], rwkv_lnx_b[l])
        y_a = _rmsnorm(_swa_sinks(p_a, attn_sinks[l]), attn_norm_g[l])
        y_g = _rmsnorm(_chunk_gmlp(p_g, gm_ln_g[l], gm_ln_b[l], gm_ws[l], gm_bs[l]), gm_norm_g[l])
        x = x + jnp.concatenate([y_r, y_a, y_g], axis=-1) @ w_out[l]
        h = _rmsnorm(x, ln2_g[l])
        x = x + jnp.square(jax.nn.relu(h @ w_ffn_up[l])) @ w_ffn_down[l]
    return _rmsnorm(x, lnf_g)
```

```python
import functools

import jax
import jax.numpy as jnp
from jax import lax
from jax.experimental import pallas as pl
from jax.experimental.pallas import tpu as pltpu

F32 = jnp.float32
BF16 = jnp.bfloat16

D_MODEL = 2048
SEQ = 8192
DEPTH = 4
HEAD_DIM = 64
RWKV_HEADS = 12
RWKV_WIDTH = RWKV_HEADS * HEAD_DIM
DECAY_LORA = 64
ICLR_LORA = 64
GATE_LORA = 128
LORA_IN = DECAY_LORA + ICLR_LORA
ATTN_Q_HEADS = 12
ATTN_KV_HEADS = 4
ATTN_GROUP = ATTN_Q_HEADS // ATTN_KV_HEADS
ATTN_WIDTH = ATTN_Q_HEADS * HEAD_DIM
ATTN_KV_WIDTH = ATTN_KV_HEADS * HEAD_DIM
WINDOW = 128
GM_HEADS = 4
GM_HEAD_DIM = 128
GM_WIDTH = GM_HEADS * GM_HEAD_DIM
GM_CHUNK = 128
MIX_WIDTH = RWKV_WIDTH + ATTN_WIDTH + GM_WIDTH
D_FF = 4 * D_MODEL
RWKV_COLS = 3 * RWKV_WIDTH + DECAY_LORA + ICLR_LORA + GATE_LORA
ATTN_COLS = ATTN_WIDTH + 2 * ATTN_KV_WIDTH
GM_COLS = 2 * GM_WIDTH
NORM_EPS = 1e-6
GN_EPS = 64e-5

RWKV_CHUNK = 64
RWKV_GROUP_HEADS = 4
RWKV_GROUP_WIDTH = RWKV_GROUP_HEADS * HEAD_DIM
RWKV_GROUPS = RWKV_HEADS // RWKV_GROUP_HEADS
RWKV_DOUBLINGS = 5

V7X_VMEM_LIMIT = 56 * 1024 * 1024

_NN = (((1,), (0,)), ((), ()))
_NT = (((1,), (1,)), ((), ()))
_TN = (((0,), (0,)), ((), ()))


def _dot(a, b, dims=_NN):
    return lax.dot_general(a, b, dims, preferred_element_type=F32)


def _split_hi_lo(x):
    hi = x.astype(BF16)
    lo = (x - hi.astype(F32)).astype(BF16)
    return hi, lo


def _mm(a, b, dims=_NN):
    return _dot(a.astype(BF16), b.astype(BF16), dims)


def _mm3(a, b, dims=_NN):
    ah, al = _split_hi_lo(a)
    bh, bl = _split_hi_lo(b)
    return _dot(ah, bh, dims) + (_dot(ah, bl, dims) + _dot(al, bh, dims))


def _mm_exact_rhs(a, b_bf16):
    ah, al = _split_hi_lo(a)
    return _dot(ah, b_bf16) + _dot(al, b_bf16)


def _rmsnorm(x, g):
    return x * lax.rsqrt(jnp.mean(x * x, axis=-1, keepdims=True) + NORM_EPS) * g


def _norm_matmul_kernel(x_ref, g_ref, w_ref, o_ref, h_ref):
    @pl.when(pl.program_id(1) == 0)
    def _():
        h_ref[...] = _rmsnorm(x_ref[...], g_ref[...]).astype(BF16)

    o_ref[...] = _dot(h_ref[...], w_ref[...]).astype(o_ref.dtype)


def _norm_matmul(x, g, w, *, tm, tn, out_dtype, name):
    m, d = x.shape
    n = w.shape[1]
    return pl.pallas_call(
        _norm_matmul_kernel,
        grid=(m // tm, n // tn),
        in_specs=[
            pl.BlockSpec((tm, d), lambda i, j: (i, 0)),
            pl.BlockSpec((1, d), lambda i, j: (0, 0)),
            pl.BlockSpec((d, tn), lambda i, j: (0, j)),
        ],
        out_specs=pl.BlockSpec((tm, tn), lambda i, j: (i, j)),
        out_shape=jax.ShapeDtypeStruct((m, n), out_dtype),
        scratch_shapes=[pltpu.VMEM((tm, d), BF16)],
        compiler_params=pltpu.CompilerParams(
            dimension_semantics=("parallel", "arbitrary"), vmem_limit_bytes=V7X_VMEM_LIMIT),
        name=name,
    )(x, g, w)


def _rwkv_group_scan(r, logd, k, v, a, b, s_ref, bdmask, tril_ones, strict, incl, eye4):
    c = RWKV_CHUNK

    def bd(x):
        return jnp.where(bdmask, jnp.concatenate([x] * RWKV_GROUP_HEADS, axis=0), 0.0)

    dh, dl = _split_hi_lo(logd)
    cum = _dot(tril_ones, dh) + _dot(tril_ones, dl)
    cum_end = cum[c - 1:c, :]
    at = a * jnp.exp(cum - logd)
    rt = r * jnp.exp(cum)
    inv = jnp.exp(-cum)
    bt = b * inv
    kt = k * inv
    to_end = jnp.exp(cum_end - cum)
    b_end = b * to_end
    k_end = k * to_end

    x = jnp.concatenate([at, rt], axis=0)
    ab = _mm(x, bd(bt), _NT)
    ak = _mm(x, bd(kt), _NT)
    a_ab = jnp.where(strict, ab[:c], 0.0)
    a_rb = jnp.where(incl, ab[c:], 0.0)
    a_ak = jnp.where(strict, ak[:c], 0.0)
    a_rk = jnp.where(incl, ak[c:], 0.0)

    t_inv = eye4 + a_ab
    p = _mm3(a_ab, bd(a_ab))
    for i in range(RWKV_DOUBLINGS):
        if i + 1 < RWKV_DOUBLINGS:
            tp = _mm3(jnp.concatenate([t_inv, p], axis=0), bd(p))
            t_inv = t_inv + tp[:c]
            p = tp[c:]
        else:
            t_inv = t_inv + _mm3(t_inv, bd(p))

    s0 = s_ref[...]
    xs = _mm3(x, s0, _NT)
    av = _mm(jnp.concatenate([a_ak, a_rk], axis=0), bd(v))
    u = _mm3(t_inv, bd(xs[:c] + av[:c]))
    y = xs[c:] + _mm(a_rb, bd(u)) + av[c:]
    upd = _mm3(jnp.concatenate([u, v], axis=0), jnp.concatenate([b_end, k_end], axis=0), _TN)
    s_ref[...] = s0 * jnp.exp(cum_end) + jnp.where(bdmask, upd, 0.0)
    return y


def _rwkv_kernel(p_ref, mu_ref, w0_ref, wdec_ref, a0_ref, wa_ref, wg_ref, kk_ref, ka_ref, rk_ref,
                 lng_ref, lnb_ref, o_ref, prev_ref, s_ref):
    c, gw = RWKV_CHUNK, RWKV_GROUP_WIDTH

    @pl.when(pl.program_id(0) == 0)
    def _():
        prev_ref[...] = jnp.zeros_like(prev_ref)
        s_ref[...] = jnp.zeros_like(s_ref)

    p = p_ref[...]
    row = lax.broadcasted_iota(jnp.int32, p.shape, 0)
    shifted = jnp.where(row == 0, prev_ref[...], pltpu.roll(p, 1, axis=0))
    prev_ref[...] = p[c - 1:c, :]
    p = p + (shifted - p) * mu_ref[...]

    w = RWKV_WIDTH
    r = p[:, 0:w]
    k = p[:, w:2 * w]
    v = p[:, 2 * w:3 * w]
    x_lora = p[:, 3 * w:3 * w + LORA_IN]
    x_gate = p[:, 3 * w + LORA_IN:]

    dec = -(jax.nn.softplus(-(w0_ref[...] + _mm(jnp.tanh(x_lora), wdec_ref[...])))) - 0.5
    logd = -jnp.exp(dec)
    alpha = jax.nn.sigmoid(a0_ref[...] + _mm(x_lora, wa_ref[...]))
    gate = _mm(jax.nn.sigmoid(x_gate), wg_ref[...])

    rowh = lax.broadcasted_iota(jnp.int32, (gw, gw), 0) >> 6
    colh = lax.broadcasted_iota(jnp.int32, (gw, gw), 1) >> 6
    bdmask = rowh == colh
    head_ones = bdmask.astype(BF16)
    ti = lax.broadcasted_iota(jnp.int32, (c, c), 0)
    si = lax.broadcasted_iota(jnp.int32, (c, c), 1)
    tril_ones = (si <= ti).astype(BF16)
    t2 = lax.broadcasted_iota(jnp.int32, (c, gw), 0)
    s2 = lax.broadcasted_iota(jnp.int32, (c, gw), 1) & (c - 1)
    strict = s2 < t2
    incl = s2 <= t2
    eye4 = (s2 == t2).astype(F32)

    def head_sum(x):
        return _mm_exact_rhs(x, head_ones)

    kkw, kaw, rkw = kk_ref[...], ka_ref[...], rk_ref[...]
    lng, lnb = lng_ref[...], lnb_ref[...]
    for g in range(RWKV_GROUPS):
        sl = slice(g * gw, (g + 1) * gw)
        rg, kg, vg, ag = r[:, sl], k[:, sl], v[:, sl], alpha[:, sl]
        kk = kg * kkw[:, sl]
        kk = kk / jnp.maximum(jnp.sqrt(head_sum(kk * kk)), 1e-12)
        k2 = kg * (1.0 + (ag - 1.0) * kaw[:, sl])
        y = _rwkv_group_scan(rg, logd[:, sl], k2, vg, -kk, kk * ag, s_ref.at[g],
                             bdmask, tril_ones, strict, incl, eye4)
        mean = head_sum(y) * (1.0 / HEAD_DIM)
        yc = y - mean
        var = head_sum(yc * yc) * (1.0 / HEAD_DIM)
        y = yc * lax.rsqrt(var + GN_EPS) * lng[:, sl] + lnb[:, sl]
        y = y + head_sum(rg * k2 * rkw[:, sl]) * vg
        o_ref[:, sl] = (y * gate[:, sl]).astype(o_ref.dtype)


def _rwkv_mix(p_r, mu, w0, wdec, a0, wa, wg, k_k, k_a, r_k, lnx_g, lnx_b):
    s = p_r.shape[0]
    c = RWKV_CHUNK
    row = lambda n: pl.BlockSpec((1, n), lambda i: (0, 0))
    full = lambda a: pl.BlockSpec(a.shape, lambda i: (0, 0))
    return pl.pallas_call(
        _rwkv_kernel,
        grid=(s // c,),
        in_specs=[
            pl.BlockSpec((c, RWKV_COLS), lambda i: (i, 0)),
            row(RWKV_COLS), row(RWKV_WIDTH), full(wdec), row(RWKV_WIDTH), full(wa), full(wg),
            row(RWKV_WIDTH), row(RWKV_WIDTH), row(RWKV_WIDTH), row(RWKV_WIDTH), row(RWKV_WIDTH),
        ],
        out_specs=pl.BlockSpec((c, RWKV_WIDTH), lambda i: (i, 0)),
        out_shape=jax.ShapeDtypeStruct((s, RWKV_WIDTH), BF16),
        scratch_shapes=[
            pltpu.VMEM((1, RWKV_COLS), F32),
            pltpu.VMEM((RWKV_GROUPS, RWKV_GROUP_WIDTH, RWKV_GROUP_WIDTH), F32),
        ],
        compiler_params=pltpu.CompilerParams(
            dimension_semantics=("arbitrary",), vmem_limit_bytes=V7X_VMEM_LIMIT),
        name="rwkv7_mix",
    )(p_r, mu, w0, wdec, a0, wa, wg, k_k, k_a, r_k, lnx_g, lnx_b)


def _swa_kernel(sink_ref, q_ref, kc_ref, vc_ref, kp_ref, vp_ref, g_ref, o_ref):
    w, hd = WINDOW, HEAD_DIM
    blk = pl.program_id(0)
    qi = lax.broadcasted_iota(jnp.int32, (w, 2 * w), 0)
    kj = lax.broadcasted_iota(jnp.int32, (w, 2 * w), 1)
    mask = (kj > qi) & (kj <= qi + w) & ((kj >= w) | (blk > 0))
    mask = jnp.concatenate([mask] * ATTN_GROUP, axis=0)
    grp_row = lax.broadcasted_iota(jnp.int32, (ATTN_GROUP * w, 1), 0) >> 7
    q = q_ref[...]
    outs = [None] * ATTN_Q_HEADS
    for kv in range(ATTN_KV_HEADS):
        ksl = slice(kv * hd, (kv + 1) * hd)
        kb = jnp.concatenate([kp_ref[:, ksl], kc_ref[:, ksl]], axis=0)
        vb = jnp.concatenate([vp_ref[:, ksl], vc_ref[:, ksl]], axis=0)
        heads = [kv * ATTN_GROUP + j for j in range(ATTN_GROUP)]
        qs = jnp.concatenate([q[:, h * hd:(h + 1) * hd] for h in heads], axis=0)
        s = _dot(qs, kb, _NT) * (hd ** -0.5)
        s = jnp.where(mask, s, -jnp.inf)
        sink = jnp.full((ATTN_GROUP * w, 1), sink_ref[heads[0]], F32)
        for j in range(1, ATTN_GROUP):
            sink = jnp.where(grp_row == j, sink_ref[heads[j]], sink)
        m = jnp.maximum(jnp.max(s, axis=-1, keepdims=True), sink)
        e = jnp.exp(s - m)
        denom = jnp.sum(e, axis=-1, keepdims=True) + jnp.exp(sink - m)
        o = _dot(e.astype(BF16), vb) / denom
        for j, h in enumerate(heads):
            outs[h] = o[j * w:(j + 1) * w, :]
    y = jnp.concatenate(outs, axis=-1)
    o_ref[...] = _rmsnorm(y, g_ref[...]).astype(o_ref.dtype)


def _swa_mix(p_a, sinks, norm_g):
    s = p_a.shape[0]
    w = WINDOW
    kcol = ATTN_WIDTH // ATTN_KV_WIDTH
    return pl.pallas_call(
        _swa_kernel,
        grid=(s // w,),
        in_specs=[
            pl.BlockSpec(memory_space=pltpu.SMEM),
            pl.BlockSpec((w, ATTN_WIDTH), lambda i: (i, 0)),
            pl.BlockSpec((w, ATTN_KV_WIDTH), lambda i: (i, kcol)),
            pl.BlockSpec((w, ATTN_KV_WIDTH), lambda i: (i, kcol + 1)),
            pl.BlockSpec((w, ATTN_KV_WIDTH), lambda i: (jnp.maximum(i - 1, 0), kcol)),
            pl.BlockSpec((w, ATTN_KV_WIDTH), lambda i: (jnp.maximum(i - 1, 0), kcol + 1)),
            pl.BlockSpec((1, ATTN_WIDTH), lambda i: (0, 0)),
        ],
        out_specs=pl.BlockSpec((w, ATTN_WIDTH), lambda i: (i, 0)),
        out_shape=jax.ShapeDtypeStruct((s, ATTN_WIDTH), BF16),
        compiler_params=pltpu.CompilerParams(dimension_semantics=("parallel",)),
        name="swa_sink_mix",
    )(sinks, p_a, p_a, p_a, p_a, p_a, norm_g)


def _gmlp_kernel(p_ref, lng_ref, lnb_ref, ws_ref, bias_ref, g_ref, o_ref):
    ch, hd = GM_CHUNK, GM_HEAD_DIM
    p = p_ref[...]
    z = 0.5 * p * (1.0 + lax.erf(p * (2.0 ** -0.5)))
    u = z[:, :GM_WIDTH]
    z2 = z[:, GM_WIDTH:]
    mu = jnp.mean(z2, axis=-1, keepdims=True)
    zc = z2 - mu
    var = jnp.mean(zc * zc, axis=-1, keepdims=True)
    z2 = (zc * lax.rsqrt(var + NORM_EPS) * lng_ref[...] + lnb_ref[...]).astype(BF16)
    ti = lax.broadcasted_iota(jnp.int32, (ch, ch), 0)
    si = lax.broadcasted_iota(jnp.int32, (ch, ch), 1)
    causal = si <= ti
    mixed = []
    for h in range(GM_HEADS):
        wh = jnp.where(causal, ws_ref[h], 0.0).astype(BF16)
        mixed.append(_dot(wh, z2[:, h * hd:(h + 1) * hd]))
    y = u * (jnp.concatenate(mixed, axis=-1) + bias_ref[...])
    o_ref[...] = _rmsnorm(y, g_ref[...]).astype(o_ref.dtype)


def _gmlp_mix(p_g, ln_g, ln_b, ws, bias_full, norm_g):
    s = p_g.shape[0]
    ch = GM_CHUNK
    row = pl.BlockSpec((1, GM_WIDTH), lambda i: (0, 0))
    return pl.pallas_call(
        _gmlp_kernel,
        grid=(s // ch,),
        in_specs=[
            pl.BlockSpec((ch, GM_COLS), lambda i: (i, 0)),
            row, row,
            pl.BlockSpec((GM_HEADS, ch, ch), lambda i: (0, 0, 0)),
            pl.BlockSpec((ch, GM_WIDTH), lambda i: (0, 0)),
            row,
        ],
        out_specs=pl.BlockSpec((ch, GM_WIDTH), lambda i: (i, 0)),
        out_shape=jax.ShapeDtypeStruct((s, GM_WIDTH), BF16),
        compiler_params=pltpu.CompilerParams(dimension_semantics=("parallel",)),
        name="chunk_gmlp_mix",
    )(p_g, ln_g, ln_b, ws, bias_full, norm_g)


def _out_proj_kernel(yr_ref, ya_ref, yg_ref, wr_ref, wa_ref, wg_ref, x_ref, o_ref):
    acc = _dot(yr_ref[...], wr_ref[...]) + _dot(ya_ref[...], wa_ref[...]) + _dot(yg_ref[...], wg_ref[...])
    o_ref[...] = x_ref[...] + acc


def _out_proj(y_r, y_a, y_g, w_r, w_a, w_g, x, *, tm, tn):
    m, d = x.shape
    lhs = lambda n: pl.BlockSpec((tm, n), lambda i, j: (i, 0))
    rhs = lambda n: pl.BlockSpec((n, tn), lambda i, j: (0, j))
    return pl.pallas_call(
        _out_proj_kernel,
        grid=(m // tm, d // tn),
        in_specs=[lhs(RWKV_WIDTH), lhs(ATTN_WIDTH), lhs(GM_WIDTH),
                  rhs(RWKV_WIDTH), rhs(ATTN_WIDTH), rhs(GM_WIDTH),
                  pl.BlockSpec((tm, tn), lambda i, j: (i, j))],
        out_specs=pl.BlockSpec((tm, tn), lambda i, j: (i, j)),
        out_shape=jax.ShapeDtypeStruct((m, d), F32),
        compiler_params=pltpu.CompilerParams(
            dimension_semantics=("parallel", "parallel"), vmem_limit_bytes=V7X_VMEM_LIMIT),
        name="out_proj_residual",
    )(y_r, y_a, y_g, w_r, w_a, w_g, x)


def _ffn_kernel(x_ref, g_ref, up_ref, down_ref, o_ref, h_ref):
    @pl.when(pl.program_id(1) == 0)
    def _():
        x = x_ref[...]
        h_ref[...] = _rmsnorm(x, g_ref[...]).astype(BF16)
        o_ref[...] = x

    a = jnp.maximum(_dot(h_ref[...], up_ref[...]), 0.0)
    o_ref[...] += _dot((a * a).astype(BF16), down_ref[...])


def _ffn(x, g, w_up, w_down, *, tm, tf):
    m, d = x.shape
    f = w_up.shape[1]
    return pl.pallas_call(
        _ffn_kernel,
        grid=(m // tm, f // tf),
        in_specs=[
            pl.BlockSpec((tm, d), lambda i, j: (i, 0)),
            pl.BlockSpec((1, d), lambda i, j: (0, 0)),
            pl.BlockSpec((d, tf), lambda i, j: (0, j)),
            pl.BlockSpec((tf, d), lambda i, j: (j, 0)),
        ],
        out_specs=pl.BlockSpec((tm, d), lambda i, j: (i, 0)),
        out_shape=jax.ShapeDtypeStruct((m, d), F32),
        scratch_shapes=[pltpu.VMEM((tm, d), BF16)],
        compiler_params=pltpu.CompilerParams(
            dimension_semantics=("parallel", "arbitrary"), vmem_limit_bytes=V7X_VMEM_LIMIT),
        name="relu2_mlp_residual",
    )(x, g, w_up, w_down)


def _final_norm_kernel(x_ref, g_ref, o_ref):
    o_ref[...] = _rmsnorm(x_ref[...], g_ref[...])


def _final_norm(x, g, *, tm):
    m, d = x.shape
    return pl.pallas_call(
        _final_norm_kernel,
        grid=(m // tm,),
        in_specs=[pl.BlockSpec((tm, d), lambda i: (i, 0)), pl.BlockSpec((1, d), lambda i: (0, 0))],
        out_specs=pl.BlockSpec((tm, d), lambda i: (i, 0)),
        out_shape=jax.ShapeDtypeStruct((m, d), F32),
        compiler_params=pltpu.CompilerParams(dimension_semantics=("parallel",)),
        name="final_rmsnorm",
    )(x, g)


def kernel(x, ln1_g, w_in, rwkv_mu, rwkv_w0, rwkv_decay_up, rwkv_a0, rwkv_a_up, rwkv_g_up, rwkv_k_k,
           rwkv_k_a, rwkv_r_k, rwkv_lnx_g, rwkv_lnx_b, attn_sinks, attn_norm_g, gm_ln_g, gm_ln_b, gm_ws,
           gm_bs, gm_norm_g, w_out, ln2_g, w_ffn_up, w_ffn_down, lnf_g):
    batch, seq, d = x.shape
    assert (batch, seq, d) == (1, SEQ, D_MODEL)
    xs = x.reshape(seq, d)
    row = lambda a: a.reshape(1, -1)
    zeros_lora = jnp.zeros((LORA_IN - DECAY_LORA, RWKV_WIDTH), F32)
    for l in range(DEPTH):
        w_l = w_in[l].astype(BF16)
        g1 = row(ln1_g[l])
        p_r = _norm_matmul(xs, g1, w_l[:, :RWKV_COLS], tm=1024, tn=RWKV_COLS // 2, out_dtype=F32,
                           name="norm_in_proj_rwkv")
        p_a = _norm_matmul(xs, g1, w_l[:, RWKV_COLS:RWKV_COLS + ATTN_COLS], tm=1024, tn=ATTN_COLS,
                           out_dtype=BF16, name="norm_in_proj_attn")
        p_g = _norm_matmul(xs, g1, w_l[:, RWKV_COLS + ATTN_COLS:], tm=1024, tn=GM_COLS, out_dtype=F32,
                           name="norm_in_proj_gmlp")

        wdec = jnp.concatenate([rwkv_decay_up[l], zeros_lora], axis=0).astype(BF16)
        wa = jnp.concatenate([zeros_lora, rwkv_a_up[l]], axis=0).astype(BF16)
        y_r = _rwkv_mix(p_r, row(rwkv_mu[l]), row(rwkv_w0[l]), wdec, row(rwkv_a0[l]), wa,
                        rwkv_g_up[l].astype(BF16), row(rwkv_k_k[l]), row(rwkv_k_a[l]), row(rwkv_r_k[l]),
                        row(rwkv_lnx_g[l]), row(rwkv_lnx_b[l]))
        y_a = _swa_mix(p_a, attn_sinks[l], row(attn_norm_g[l]))
        bias_full = jnp.repeat(gm_bs[l].T, GM_HEAD_DIM, axis=1)
        y_g = _gmlp_mix(p_g, row(gm_ln_g[l]), row(gm_ln_b[l]), gm_ws[l], bias_full, row(gm_norm_g[l]))

        wo = w_out[l].astype(BF16)
        xs = _out_proj(y_r, y_a, y_g, wo[:RWKV_WIDTH], wo[RWKV_WIDTH:RWKV_WIDTH + ATTN_WIDTH],
                       wo[RWKV_WIDTH + ATTN_WIDTH:], xs, tm=512, tn=1024)
        xs = _ffn(xs, row(ln2_g[l]), w_ffn_up[l].astype(BF16), w_ffn_down[l].astype(BF16), tm=512, tf=1024)
    return _final_norm(xs, row(lnf_g), tm=512).reshape(batch, seq, d)
```

```python
import jax
import jax.numpy as jnp
from jax import lax
from jax.experimental import pallas as pl
from jax.experimental.pallas import tpu as pltpu

F32 = jnp.float32
BF16 = jnp.bfloat16

D_MODEL = 2048
SEQ = 8192
DEPTH = 4
HEAD_DIM = 64
RWKV_HEADS = 12
RWKV_WIDTH = RWKV_HEADS * HEAD_DIM
DECAY_LORA = 64
ICLR_LORA = 64
GATE_LORA = 128
LORA_IN = DECAY_LORA + ICLR_LORA
ATTN_Q_HEADS = 12
ATTN_KV_HEADS = 4
ATTN_GROUP = ATTN_Q_HEADS // ATTN_KV_HEADS
ATTN_WIDTH = ATTN_Q_HEADS * HEAD_DIM
ATTN_KV_WIDTH = ATTN_KV_HEADS * HEAD_DIM
WINDOW = 128
GM_HEADS = 4
GM_HEAD_DIM = 128
GM_WIDTH = GM_HEADS * GM_HEAD_DIM
GM_CHUNK = 128
MIX_WIDTH = RWKV_WIDTH + ATTN_WIDTH + GM_WIDTH
D_FF = 4 * D_MODEL
RWKV_COLS = 3 * RWKV_WIDTH + DECAY_LORA + ICLR_LORA + GATE_LORA
ATTN_COLS = ATTN_WIDTH + 2 * ATTN_KV_WIDTH
GM_COLS = 2 * GM_WIDTH
NORM_EPS = 1e-6
GN_EPS = 64e-5

RWKV_CHUNK = 64
RWKV_GROUP_HEADS = 4
RWKV_GROUP_WIDTH = RWKV_GROUP_HEADS * HEAD_DIM
RWKV_GROUPS = RWKV_HEADS // RWKV_GROUP_HEADS
RWKV_STEP_CHUNKS = 2
RWKV_STEP_ROWS = RWKV_STEP_CHUNKS * RWKV_CHUNK
RWKV_CHUNK_LOG2 = 6
RWKV_INV_BASE_LOG2 = 3

V7X_VMEM_LIMIT = 56 * 1024 * 1024

_NN = (((1,), (0,)), ((), ()))
_NT = (((1,), (1,)), ((), ()))
_TN = (((0,), (0,)), ((), ()))


def _dot(a, b, dims=_NN):
    return lax.dot_general(a, b, dims, preferred_element_type=F32)


def _split_hi_lo(x):
    hi = x.astype(BF16)
    lo = (x - hi.astype(F32)).astype(BF16)
    return hi, lo


def _mm(a, b, dims=_NN):
    return _dot(a.astype(BF16), b.astype(BF16), dims)


def _rmsnorm(x, g):
    return x * lax.rsqrt(jnp.mean(x * x, axis=-1, keepdims=True) + NORM_EPS) * g


def _norm_matmul_kernel(x_ref, g_ref, w_ref, o_ref, h_ref):
    @pl.when(pl.program_id(1) == 0)
    def _():
        h_ref[...] = _rmsnorm(x_ref[...], g_ref[...]).astype(BF16)

    o_ref[...] = _dot(h_ref[...], w_ref[...]).astype(o_ref.dtype)


def _norm_matmul(x, g, w, layer, *, col_block, n, tm, tn, out_dtype, name):
    m, d = x.shape
    return pl.pallas_call(
        _norm_matmul_kernel,
        grid=(m // tm, n // tn),
        in_specs=[
            pl.BlockSpec((tm, d), lambda i, j: (i, 0)),
            pl.BlockSpec((1, d), lambda i, j: (0, 0)),
            pl.BlockSpec((None, d, tn), lambda i, j: (layer, 0, col_block + j)),
        ],
        out_specs=pl.BlockSpec((tm, tn), lambda i, j: (i, j)),
        out_shape=jax.ShapeDtypeStruct((m, n), out_dtype),
        scratch_shapes=[pltpu.VMEM((tm, d), BF16)],
        compiler_params=pltpu.CompilerParams(
            dimension_semantics=("parallel", "arbitrary"), vmem_limit_bytes=V7X_VMEM_LIMIT),
        name=name,
    )(x, g, w)


def _rwkv_kernel(p_ref, mu_ref, w0_ref, wdec_ref, a0_ref, wa_ref, wg_ref, kk_ref, ka_ref, rk_ref,
                 lng_ref, lnb_ref, o_ref, prev_ref, s_ref):
    c, gw, rows = RWKV_CHUNK, RWKV_GROUP_WIDTH, RWKV_STEP_ROWS
    groups = range(RWKV_GROUPS)
    units = [(ci, g) for ci in range(RWKV_STEP_CHUNKS) for g in groups]

    @pl.when(pl.program_id(0) == 0)
    def _():
        prev_ref[...] = jnp.zeros_like(prev_ref)
        s_ref[...] = jnp.zeros_like(s_ref)

    p = p_ref[...]
    rolled = pltpu.roll(p, 1, axis=0)
    first_row = lax.broadcasted_iota(jnp.int32, (8, RWKV_COLS), 0) == 0
    shifted = jnp.concatenate([jnp.where(first_row, prev_ref[...], rolled[:8]), rolled[8:]], axis=0)
    prev_ref[...] = p[rows - 1:rows, :]
    p = p + (shifted - p) * mu_ref[...]

    w = RWKV_WIDTH
    x_lora = p[:, 3 * w:3 * w + LORA_IN]
    x_gate = p[:, 3 * w + LORA_IN:]
    dec = -(jax.nn.softplus(-(w0_ref[...] + _mm(jnp.tanh(x_lora), wdec_ref[...])))) - 0.5
    logd_all = -jnp.exp(dec)
    alpha_all = jax.nn.sigmoid(a0_ref[...] + _mm(x_lora, wa_ref[...]))
    gate_all = _mm(jax.nn.sigmoid(x_gate), wg_ref[...])

    rowh = lax.broadcasted_iota(jnp.int32, (gw, gw), 0) >> 6
    colh = lax.broadcasted_iota(jnp.int32, (gw, gw), 1) >> 6
    bdmask = rowh == colh
    head_ones = bdmask.astype(BF16)
    ti = lax.broadcasted_iota(jnp.int32, (rows, rows), 0)
    si = lax.broadcasted_iota(jnp.int32, (rows, rows), 1)
    tril_ones = ((si <= ti) & ((si >> 6) == (ti >> 6))).astype(BF16)
    t2 = lax.broadcasted_iota(jnp.int32, (c, gw), 0)
    s2 = lax.broadcasted_iota(jnp.int32, (c, gw), 1) & (c - 1)
    strict = s2 < t2
    incl = s2 <= t2
    eye4 = (s2 == t2).astype(F32)
    lane = lax.broadcasted_iota(jnp.int32, (c, 2 * HEAD_DIM), 1)
    even_head = (lane < HEAD_DIM).astype(BF16)
    odd_head = (lane >= HEAD_DIM).astype(BF16)
    zero_slab = jnp.zeros((c, 2 * HEAD_DIM), BF16)

    def head_sum(x):
        return _dot(x.astype(BF16), head_ones)

    def bd(x):
        x = x.astype(BF16)
        lo, hi = x[:, :2 * HEAD_DIM], x[:, 2 * HEAD_DIM:]
        return jnp.concatenate([
            jnp.concatenate([lo * even_head, zero_slab], axis=1),
            jnp.concatenate([lo * odd_head, zero_slab], axis=1),
            jnp.concatenate([zero_slab, hi * even_head], axis=1),
            jnp.concatenate([zero_slab, hi * odd_head], axis=1)], axis=0)

    def cols(x, g):
        return x[:, g * gw:(g + 1) * gw]

    def chunk(x, ci):
        return x[ci * c:(ci + 1) * c, :]

    r = [cols(p[:, 0:w], g) for g in groups]
    k = [cols(p[:, w:2 * w], g) for g in groups]
    v = [cols(p[:, 2 * w:3 * w], g) for g in groups]
    alpha = [cols(alpha_all, g) for g in groups]
    logd = [cols(logd_all, g) for g in groups]
    kk = [k[g] * cols(kk_ref[...], g) for g in groups]
    kk_sq = [head_sum(kk[g] * kk[g]) for g in groups]
    cum = []
    for g in groups:
        dh, dl = _split_hi_lo(logd[g])
        cum.append(_dot(tril_ones, dh) + _dot(tril_ones, dl))
    k2 = [k[g] * (1.0 + (alpha[g] - 1.0) * cols(ka_ref[...], g)) for g in groups]
    bonus = [head_sum(r[g] * k2[g] * cols(rk_ref[...], g)) * v[g] for g in groups]
    kkn = [kk[g] / jnp.maximum(jnp.sqrt(kk_sq[g]), 1e-12) for g in groups]
    b_all = [kkn[g] * alpha[g] for g in groups]
    at_all = [-kkn[g] * jnp.exp(cum[g] - logd[g]) for g in groups]
    rt_all = [r[g] * jnp.exp(cum[g]) for g in groups]
    inv_all = [jnp.exp(-cum[g]) for g in groups]
    bt_all = [b_all[g] * inv_all[g] for g in groups]
    kt_all = [k2[g] * inv_all[g] for g in groups]

    x, ab, ak = {}, {}, {}
    for (ci, g) in units:
        x[ci, g] = jnp.concatenate([chunk(at_all[g], ci), chunk(rt_all[g], ci)], axis=0)
        ab[ci, g] = _mm(x[ci, g], bd(chunk(bt_all[g], ci)), _NT)
        ak[ci, g] = _mm(x[ci, g], bd(chunk(kt_all[g], ci)), _NT)
    a_ab = {u_: jnp.where(strict, ab[u_][:c], 0.0) for u_ in units}
    a_rb = {u_: jnp.where(incl, ab[u_][c:], 0.0) for u_ in units}
    a_ak = {u_: jnp.where(strict, ak[u_][:c], 0.0) for u_ in units}
    a_rk = {u_: jnp.where(incl, ak[u_][c:], 0.0) for u_ in units}

    blk_t, blk_s = t2 >> 3, s2 >> 3
    a_dg = {u_: jnp.where(blk_t == blk_s, a_ab[u_], 0.0) for u_ in units}
    t_inv = {u_: eye4 + a_dg[u_] for u_ in units}
    pw = {u_: _mm(a_dg[u_], bd(a_dg[u_])) for u_ in units}
    av = {(ci, g): _mm(jnp.concatenate([a_ak[ci, g], a_rk[ci, g]], axis=0), bd(chunk(v[g], ci)))
          for (ci, g) in units}
    tp = {u_: _mm(jnp.concatenate([t_inv[u_], pw[u_]], axis=0), bd(pw[u_])) for u_ in units}
    t_inv = {u_: t_inv[u_] + tp[u_][:c] for u_ in units}
    t_inv = {u_: t_inv[u_] + _mm(t_inv[u_], bd(tp[u_][c:])) for u_ in units}
    for shift in range(RWKV_INV_BASE_LOG2, RWKV_CHUNK_LOG2):
        couple = ((t2 >> (shift + 1)) == (s2 >> (shift + 1))) & ((t2 >> shift) != (s2 >> shift))
        de = {u_: _mm(t_inv[u_], bd(jnp.where(couple, a_ab[u_], 0.0))) for u_ in units}
        t_inv = {u_: t_inv[u_] + _mm(de[u_], bd(t_inv[u_])) for u_ in units}

    state = [s_ref[g] for g in groups]
    y = {}
    for ci in range(RWKV_STEP_CHUNKS):
        xs = [_mm(x[ci, g], state[g], _NT) for g in groups]
        u = [_mm(t_inv[ci, g], bd(xs[g][:c] + av[ci, g][:c])) for g in groups]
        for g in groups:
            y[ci, g] = xs[g][c:] + _mm(a_rb[ci, g], bd(u[g])) + av[ci, g][c:]
        for g in groups:
            cum_end = chunk(cum[g], ci)[c - 1:c, :]
            to_end = jnp.exp(cum_end - chunk(cum[g], ci))
            b_end = chunk(b_all[g], ci) * to_end
            k_end = chunk(k2[g], ci) * to_end
            upd = _mm(jnp.concatenate([u[g], chunk(v[g], ci)], axis=0),
                      jnp.concatenate([b_end, k_end], axis=0), _TN)
            state[g] = state[g] * jnp.exp(cum_end) + jnp.where(bdmask, upd, 0.0)
    for g in groups:
        s_ref[g] = state[g]

    yf = [jnp.concatenate([y[ci, g] for ci in range(RWKV_STEP_CHUNKS)], axis=0) for g in groups]
    mean = [head_sum(yf[g]) * (1.0 / HEAD_DIM) for g in groups]
    yc = [yf[g] - mean[g] for g in groups]
    var = [head_sum(yc[g] * yc[g]) * (1.0 / HEAD_DIM) for g in groups]
    for g in groups:
        yn = yc[g] * lax.rsqrt(var[g] + GN_EPS) * cols(lng_ref[...], g) + cols(lnb_ref[...], g)
        o_ref[:, g * gw:(g + 1) * gw] = ((yn + bonus[g]) * cols(gate_all, g)).astype(o_ref.dtype)


def _rwkv_mix(p_r, mu, w0, wdec, a0, wa, wg, k_k, k_a, r_k, lnx_g, lnx_b):
    s = p_r.shape[0]
    rows = RWKV_STEP_ROWS
    row = lambda n: pl.BlockSpec((1, n), lambda i: (0, 0))
    full = lambda a: pl.BlockSpec(a.shape, lambda i: (0, 0))
    return pl.pallas_call(
        _rwkv_kernel,
        grid=(s // rows,),
        in_specs=[
            pl.BlockSpec((rows, RWKV_COLS), lambda i: (i, 0)),
            row(RWKV_COLS), row(RWKV_WIDTH), full(wdec), row(RWKV_WIDTH), full(wa), full(wg),
            row(RWKV_WIDTH), row(RWKV_WIDTH), row(RWKV_WIDTH), row(RWKV_WIDTH), row(RWKV_WIDTH),
        ],
        out_specs=pl.BlockSpec((rows, RWKV_WIDTH), lambda i: (i, 0)),
        out_shape=jax.ShapeDtypeStruct((s, RWKV_WIDTH), BF16),
        scratch_shapes=[
            pltpu.VMEM((1, RWKV_COLS), F32),
            pltpu.VMEM((RWKV_GROUPS, RWKV_GROUP_WIDTH, RWKV_GROUP_WIDTH), F32),
        ],
        compiler_params=pltpu.CompilerParams(
            dimension_semantics=("arbitrary",), vmem_limit_bytes=V7X_VMEM_LIMIT),
        name="rwkv7_mix",
    )(p_r, mu, w0, wdec, a0, wa, wg, k_k, k_a, r_k, lnx_g, lnx_b)


def _swa_kernel(sink_ref, q_ref, kc_ref, vc_ref, kp_ref, vp_ref, g_ref, o_ref):
    w, hd = WINDOW, HEAD_DIM
    blk = pl.program_id(0)
    qi = lax.broadcasted_iota(jnp.int32, (w, 2 * w), 0)
    kj = lax.broadcasted_iota(jnp.int32, (w, 2 * w), 1)
    mask = (kj > qi) & (kj <= qi + w) & ((kj >= w) | (blk > 0))
    mask = jnp.concatenate([mask] * ATTN_GROUP, axis=0)
    grp_row = lax.broadcasted_iota(jnp.int32, (ATTN_GROUP * w, 1), 0) >> 7
    q = q_ref[...]
    outs = [None] * ATTN_Q_HEADS
    for kv in range(ATTN_KV_HEADS):
        ksl = slice(kv * hd, (kv + 1) * hd)
        kb = jnp.concatenate([kp_ref[:, ksl], kc_ref[:, ksl]], axis=0)
        vb = jnp.concatenate([vp_ref[:, ksl], vc_ref[:, ksl]], axis=0)
        heads = [kv * ATTN_GROUP + j for j in range(ATTN_GROUP)]
        qs = jnp.concatenate([q[:, h * hd:(h + 1) * hd] for h in heads], axis=0)
        s = _dot(qs, kb, _NT) * (hd ** -0.5)
        s = jnp.where(mask, s, -jnp.inf)
        sink = jnp.full((ATTN_GROUP * w, 1), sink_ref[heads[0]], F32)
        for j in range(1, ATTN_GROUP):
            sink = jnp.where(grp_row == j, sink_ref[heads[j]], sink)
        m = jnp.maximum(jnp.max(s, axis=-1, keepdims=True), sink)
        e = jnp.exp(s - m)
        denom = jnp.sum(e, axis=-1, keepdims=True) + jnp.exp(sink - m)
        o = _dot(e.astype(BF16), vb) / denom
        for j, h in enumerate(heads):
            outs[h] = o[j * w:(j + 1) * w, :]
    y = jnp.concatenate(outs, axis=-1)
    o_ref[...] = _rmsnorm(y, g_ref[...]).astype(o_ref.dtype)


def _swa_mix(p_a, sinks, norm_g):
    s = p_a.shape[0]
    w = WINDOW
    kcol = ATTN_WIDTH // ATTN_KV_WIDTH
    return pl.pallas_call(
        _swa_kernel,
        grid=(s // w,),
        in_specs=[
            pl.BlockSpec(memory_space=pltpu.SMEM),
            pl.BlockSpec((w, ATTN_WIDTH), lambda i: (i, 0)),
            pl.BlockSpec((w, ATTN_KV_WIDTH), lambda i: (i, kcol)),
            pl.BlockSpec((w, ATTN_KV_WIDTH), lambda i: (i, kcol + 1)),
            pl.BlockSpec((w, ATTN_KV_WIDTH), lambda i: (jnp.maximum(i - 1, 0), kcol)),
            pl.BlockSpec((w, ATTN_KV_WIDTH), lambda i: (jnp.maximum(i - 1, 0), kcol + 1)),
            pl.BlockSpec((1, ATTN_WIDTH), lambda i: (0, 0)),
        ],
        out_specs=pl.BlockSpec((w, ATTN_WIDTH), lambda i: (i, 0)),
        out_shape=jax.ShapeDtypeStruct((s, ATTN_WIDTH), BF16),
        compiler_params=pltpu.CompilerParams(dimension_semantics=("parallel",)),
        name="swa_sink_mix",
    )(sinks, p_a, p_a, p_a, p_a, p_a, norm_g)


def _gmlp_kernel(p_ref, lng_ref, lnb_ref, ws_ref, bias_ref, g_ref, o_ref):
    ch, hd = GM_CHUNK, GM_HEAD_DIM
    p = p_ref[...]
    z = 0.5 * p * (1.0 + lax.erf(p * (2.0 ** -0.5)))
    u = z[:, :GM_WIDTH]
    z2 = z[:, GM_WIDTH:]
    mu = jnp.mean(z2, axis=-1, keepdims=True)
    zc = z2 - mu
    var = jnp.mean(zc * zc, axis=-1, keepdims=True)
    z2 = (zc * lax.rsqrt(var + NORM_EPS) * lng_ref[...] + lnb_ref[...]).astype(BF16)
    ti = lax.broadcasted_iota(jnp.int32, (ch, ch), 0)
    si = lax.broadcasted_iota(jnp.int32, (ch, ch), 1)
    causal = si <= ti
    mixed = []
    for h in range(GM_HEADS):
        wh = jnp.where(causal, ws_ref[h], 0.0).astype(BF16)
        mixed.append(_dot(wh, z2[:, h * hd:(h + 1) * hd]))
    y = u * (jnp.concatenate(mixed, axis=-1) + bias_ref[...])
    o_ref[...] = _rmsnorm(y, g_ref[...]).astype(o_ref.dtype)


def _gmlp_mix(p_g, ln_g, ln_b, ws, bias_full, norm_g):
    s = p_g.shape[0]
    ch = GM_CHUNK
    row = pl.BlockSpec((1, GM_WIDTH), lambda i: (0, 0))
    return pl.pallas_call(
        _gmlp_kernel,
        grid=(s // ch,),
        in_specs=[
            pl.BlockSpec((ch, GM_COLS), lambda i: (i, 0)),
            row, row,
            pl.BlockSpec((GM_HEADS, ch, ch), lambda i: (0, 0, 0)),
            pl.BlockSpec((ch, GM_WIDTH), lambda i: (0, 0)),
            row,
        ],
        out_specs=pl.BlockSpec((ch, GM_WIDTH), lambda i: (i, 0)),
        out_shape=jax.ShapeDtypeStruct((s, GM_WIDTH), BF16),
        compiler_params=pltpu.CompilerParams(dimension_semantics=("parallel",)),
        name="chunk_gmlp_mix",
    )(p_g, ln_g, ln_b, ws, bias_full, norm_g)


def _out_proj_kernel(yr_ref, ya_ref, yg_ref, wr_ref, wa_ref, wg_ref, x_ref, o_ref):
    acc = _dot(yr_ref[...], wr_ref[...]) + _dot(ya_ref[...], wa_ref[...]) + _dot(yg_ref[...], wg_ref[...])
    o_ref[...] = x_ref[...] + acc


def _out_proj(y_r, y_a, y_g, w, layer, x, *, tm, tn):
    m, d = x.shape
    lhs = lambda n: pl.BlockSpec((tm, n), lambda i, j: (i, 0))
    rhs = lambda n, blk: pl.BlockSpec((None, n, tn), lambda i, j: (layer, blk, j))
    return pl.pallas_call(
        _out_proj_kernel,
        grid=(m // tm, d // tn),
        in_specs=[lhs(RWKV_WIDTH), lhs(ATTN_WIDTH), lhs(GM_WIDTH),
                  rhs(RWKV_WIDTH, 0), rhs(ATTN_WIDTH, 1), rhs(GM_WIDTH, (RWKV_WIDTH + ATTN_WIDTH) // GM_WIDTH),
                  pl.BlockSpec((tm, tn), lambda i, j: (i, j))],
        out_specs=pl.BlockSpec((tm, tn), lambda i, j: (i, j)),
        out_shape=jax.ShapeDtypeStruct((m, d), F32),
        compiler_params=pltpu.CompilerParams(
            dimension_semantics=("parallel", "parallel"), vmem_limit_bytes=V7X_VMEM_LIMIT),
        name="out_proj_residual",
    )(y_r, y_a, y_g, w, w, w, x)


def _ffn_kernel(x_ref, g_ref, up_ref, down_ref, o_ref, h_ref):
    @pl.when(pl.program_id(1) == 0)
    def _():
        x = x_ref[...]
        h_ref[...] = _rmsnorm(x, g_ref[...]).astype(BF16)
        o_ref[...] = x

    a = jnp.maximum(_dot(h_ref[...], up_ref[...]), 0.0)
    o_ref[...] += _dot((a * a).astype(BF16), down_ref[...])


def _ffn(x, g, w_up, w_down, layer, *, tm, tf):
    m, d = x.shape
    f = w_up.shape[2]
    return pl.pallas_call(
        _ffn_kernel,
        grid=(m // tm, f // tf),
        in_specs=[
            pl.BlockSpec((tm, d), lambda i, j: (i, 0)),
            pl.BlockSpec((1, d), lambda i, j: (0, 0)),
            pl.BlockSpec((None, d, tf), lambda i, j: (layer, 0, j)),
            pl.BlockSpec((None, tf, d), lambda i, j: (layer, j, 0)),
        ],
        out_specs=pl.BlockSpec((tm, d), lambda i, j: (i, 0)),
        out_shape=jax.ShapeDtypeStruct((m, d), F32),
        scratch_shapes=[pltpu.VMEM((tm, d), BF16)],
        compiler_params=pltpu.CompilerParams(
            dimension_semantics=("parallel", "arbitrary"), vmem_limit_bytes=V7X_VMEM_LIMIT),
        name="relu2_mlp_residual",
    )(x, g, w_up, w_down)


def _final_norm_kernel(x_ref, g_ref, o_ref):
    o_ref[...] = _rmsnorm(x_ref[...], g_ref[...])


def _final_norm(x, g, *, tm):
    m, d = x.shape
    return pl.pallas_call(
        _final_norm_kernel,
        grid=(m // tm,),
        in_specs=[pl.BlockSpec((tm, d), lambda i: (i, 0)), pl.BlockSpec((1, d), lambda i: (0, 0))],
        out_specs=pl.BlockSpec((tm, d), lambda i: (i, 0)),
        out_shape=jax.ShapeDtypeStruct((m, d), F32),
        compiler_params=pltpu.CompilerParams(dimension_semantics=("parallel",)),
        name="final_rmsnorm",
    )(x, g)


def kernel(x, ln1_g, w_in, rwkv_mu, rwkv_w0, rwkv_decay_up, rwkv_a0, rwkv_a_up, rwkv_g_up, rwkv_k_k,
           rwkv_k_a, rwkv_r_k, rwkv_lnx_g, rwkv_lnx_b, attn_sinks, attn_norm_g, gm_ln_g, gm_ln_b, gm_ws,
           gm_bs, gm_norm_g, w_out, ln2_g, w_ffn_up, w_ffn_down, lnf_g):
    batch, seq, d = x.shape
    assert (batch, seq, d) == (1, SEQ, D_MODEL)
    xs = x.reshape(seq, d)
    row = lambda a: a.reshape(1, -1)
    gm_start = RWKV_COLS + ATTN_COLS
    gm_pad = -gm_start % GM_COLS
    w_in_b = jnp.concatenate(
        [w_in[:, :, :gm_start], jnp.zeros((DEPTH, d, gm_pad), F32), w_in[:, :, gm_start:]], axis=2).astype(BF16)
    w_out_b = w_out.astype(BF16)
    w_up_b = w_ffn_up.astype(BF16)
    w_down_b = w_ffn_down.astype(BF16)
    zeros_lora = jnp.zeros((LORA_IN - DECAY_LORA, RWKV_WIDTH), F32)
    for l in range(DEPTH):
        g1 = row(ln1_g[l])
        p_r = _norm_matmul(xs, g1, w_in_b, l, col_block=0, n=RWKV_COLS, tm=1024, tn=RWKV_COLS // 2,
                           out_dtype=F32, name="norm_in_proj_rwkv")
        p_a = _norm_matmul(xs, g1, w_in_b, l, col_block=RWKV_COLS // ATTN_COLS, n=ATTN_COLS, tm=1024,
                           tn=ATTN_COLS, out_dtype=BF16, name="norm_in_proj_attn")
        p_g = _norm_matmul(xs, g1, w_in_b, l, col_block=(gm_start + gm_pad) // GM_COLS, n=GM_COLS, tm=1024,
                           tn=GM_COLS, out_dtype=F32, name="norm_in_proj_gmlp")

        wdec = jnp.concatenate([rwkv_decay_up[l], zeros_lora], axis=0).astype(BF16)
        wa = jnp.concatenate([zeros_lora, rwkv_a_up[l]], axis=0).astype(BF16)
        y_r = _rwkv_mix(p_r, row(rwkv_mu[l]), row(rwkv_w0[l]), wdec, row(rwkv_a0[l]), wa,
                        rwkv_g_up[l].astype(BF16), row(rwkv_k_k[l]), row(rwkv_k_a[l]), row(rwkv_r_k[l]),
                        row(rwkv_lnx_g[l]), row(rwkv_lnx_b[l]))
        y_a = _swa_mix(p_a, attn_sinks[l], row(attn_norm_g[l]))
        bias_full = jnp.repeat(gm_bs[l].T, GM_HEAD_DIM, axis=1)
        y_g = _gmlp_mix(p_g, row(gm_ln_g[l]), row(gm_ln_b[l]), gm_ws[l], bias_full, row(gm_norm_g[l]))

        xs = _out_proj(y_r, y_a, y_g, w_out_b, l, xs, tm=1024, tn=1024)
        xs = _ffn(xs, row(ln2_g[l]), w_up_b, w_down_b, l, tm=1024, tf=512)
    return _final_norm(xs, row(lnf_g), tm=512).reshape(batch, seq, d)
```

```python
import functools

import jax
import jax.numpy as jnp
from jax import lax
from jax.experimental import pallas as pl
from jax.experimental.pallas import tpu as pltpu

F32 = jnp.float32
BF16 = jnp.bfloat16

D_MODEL = 2048
SEQ = 8192
DEPTH = 4
HEAD_DIM = 64
RWKV_HEADS = 12
RWKV_WIDTH = RWKV_HEADS * HEAD_DIM
DECAY_LORA = 64
ICLR_LORA = 64
GATE_LORA = 128
LORA_IN = DECAY_LORA + ICLR_LORA
ATTN_Q_HEADS = 12
ATTN_KV_HEADS = 4
ATTN_GROUP = ATTN_Q_HEADS // ATTN_KV_HEADS
ATTN_WIDTH = ATTN_Q_HEADS * HEAD_DIM
ATTN_KV_WIDTH = ATTN_KV_HEADS * HEAD_DIM
WINDOW = 128
GM_HEADS = 4
GM_HEAD_DIM = 128
GM_WIDTH = GM_HEADS * GM_HEAD_DIM
GM_CHUNK = 128
GM_STEP_CHUNKS = 4
MIX_WIDTH = RWKV_WIDTH + ATTN_WIDTH + GM_WIDTH
D_FF = 4 * D_MODEL
RWKV_COLS = 3 * RWKV_WIDTH + DECAY_LORA + ICLR_LORA + GATE_LORA
ATTN_COLS = ATTN_WIDTH + 2 * ATTN_KV_WIDTH
GM_COLS = 2 * GM_WIDTH
NORM_EPS = 1e-6
GN_EPS = 64e-5

RWKV_CHUNK = 64
RWKV_GROUP_HEADS = 4
RWKV_GROUP_WIDTH = RWKV_GROUP_HEADS * HEAD_DIM
RWKV_GROUPS = RWKV_HEADS // RWKV_GROUP_HEADS
RWKV_STEP_CHUNKS = 2
RWKV_STEP_ROWS = RWKV_STEP_CHUNKS * RWKV_CHUNK
RWKV_CHUNK_LOG2 = 6
RWKV_INV_BASE_LOG2 = 3

V7X_VMEM_LIMIT = 56 * 1024 * 1024

_NN = (((1,), (0,)), ((), ()))
_NT = (((1,), (1,)), ((), ()))
_TN = (((0,), (0,)), ((), ()))


def _dot(a, b, dims=_NN):
    return lax.dot_general(a, b, dims, preferred_element_type=F32)


def _split_hi_lo(x):
    hi = x.astype(BF16)
    lo = (x - hi.astype(F32)).astype(BF16)
    return hi, lo


def _mm(a, b, dims=_NN):
    return _dot(a.astype(BF16), b.astype(BF16), dims)


def _rmsnorm(x, g):
    return x * lax.rsqrt(jnp.mean(x * x, axis=-1, keepdims=True) + NORM_EPS) * g


def _norm_matmul_kernel(x_ref, g_ref, w_ref, o_ref, h_ref):
    @pl.when(pl.program_id(1) == 0)
    def _():
        h_ref[...] = _rmsnorm(x_ref[...], g_ref[...]).astype(BF16)

    o_ref[...] = _dot(h_ref[...], w_ref[...].astype(BF16)).astype(o_ref.dtype)


def _norm_matmul(x, g, w, layer, *, col_block, n, tm, tn, out_dtype, name):
    m, d = x.shape
    return pl.pallas_call(
        _norm_matmul_kernel,
        grid=(m // tm, n // tn),
        in_specs=[
            pl.BlockSpec((tm, d), lambda i, j: (i, 0)),
            pl.BlockSpec((1, d), lambda i, j: (0, 0)),
            pl.BlockSpec((None, d, tn), lambda i, j: (layer, 0, col_block + j)),
        ],
        out_specs=pl.BlockSpec((tm, tn), lambda i, j: (i, j)),
        out_shape=jax.ShapeDtypeStruct((m, n), out_dtype),
        scratch_shapes=[pltpu.VMEM((tm, d), BF16)],
        compiler_params=pltpu.CompilerParams(
            dimension_semantics=("parallel", "arbitrary"), vmem_limit_bytes=V7X_VMEM_LIMIT),
        name=name,
    )(x, g, w)


def _rwkv_kernel(p_ref, mu_ref, w0_ref, wdec_ref, a0_ref, wa_ref, wg_ref, kk_ref, ka_ref, rk_ref,
                 lng_ref, lnb_ref, o_ref, prev_ref, s_ref):
    c, gw, rows = RWKV_CHUNK, RWKV_GROUP_WIDTH, RWKV_STEP_ROWS
    groups = range(RWKV_GROUPS)
    units = [(ci, g) for ci in range(RWKV_STEP_CHUNKS) for g in groups]

    @pl.when(pl.program_id(0) == 0)
    def _():
        prev_ref[...] = jnp.zeros_like(prev_ref)
        s_ref[...] = jnp.zeros_like(s_ref)

    p = p_ref[...]
    rolled = pltpu.roll(p, 1, axis=0)
    first_row = lax.broadcasted_iota(jnp.int32, (8, RWKV_COLS), 0) == 0
    shifted = jnp.concatenate([jnp.where(first_row, prev_ref[...], rolled[:8]), rolled[8:]], axis=0)
    prev_ref[...] = p[rows - 1:rows, :]
    p = p + (shifted - p) * mu_ref[...]

    w = RWKV_WIDTH
    x_lora = p[:, 3 * w:3 * w + LORA_IN]
    x_gate = p[:, 3 * w + LORA_IN:]
    dec = -(jax.nn.softplus(-(w0_ref[...] + _mm(jnp.tanh(x_lora), wdec_ref[...])))) - 0.5
    logd_all = -jnp.exp(dec)
    alpha_all = jax.nn.sigmoid(a0_ref[...] + _mm(x_lora, wa_ref[...]))
    gate_all = _mm(jax.nn.sigmoid(x_gate), wg_ref[...])

    rowh = lax.broadcasted_iota(jnp.int32, (gw, gw), 0) >> 6
    colh = lax.broadcasted_iota(jnp.int32, (gw, gw), 1) >> 6
    bdmask = rowh == colh
    head_ones = bdmask.astype(BF16)
    ti = lax.broadcasted_iota(jnp.int32, (rows, rows), 0)
    si = lax.broadcasted_iota(jnp.int32, (rows, rows), 1)
    tril_ones = ((si <= ti) & ((si >> 6) == (ti >> 6))).astype(BF16)
    t2 = lax.broadcasted_iota(jnp.int32, (c, gw), 0)
    s2 = lax.broadcasted_iota(jnp.int32, (c, gw), 1) & (c - 1)
    strict = s2 < t2
    incl = s2 <= t2
    eye4 = (s2 == t2).astype(F32)
    lane = lax.broadcasted_iota(jnp.int32, (c, 2 * HEAD_DIM), 1)
    even_head = (lane < HEAD_DIM).astype(BF16)
    odd_head = (lane >= HEAD_DIM).astype(BF16)
    zero_slab = jnp.zeros((c, 2 * HEAD_DIM), BF16)

    def head_sum(x):
        return _dot(x.astype(BF16), head_ones)

    def bd(x):
        x = x.astype(BF16)
        lo, hi = x[:, :2 * HEAD_DIM], x[:, 2 * HEAD_DIM:]
        return jnp.concatenate([
            jnp.concatenate([lo * even_head, zero_slab], axis=1),
            jnp.concatenate([lo * odd_head, zero_slab], axis=1),
            jnp.concatenate([zero_slab, hi * even_head], axis=1),
            jnp.concatenate([zero_slab, hi * odd_head], axis=1)], axis=0)

    def cols(x, g):
        return x[:, g * gw:(g + 1) * gw]

    def chunk(x, ci):
        return x[ci * c:(ci + 1) * c, :]

    r = [cols(p[:, 0:w], g) for g in groups]
    k = [cols(p[:, w:2 * w], g) for g in groups]
    v = [cols(p[:, 2 * w:3 * w], g) for g in groups]
    alpha = [cols(alpha_all, g) for g in groups]
    logd = [cols(logd_all, g) for g in groups]
    kk = [k[g] * cols(kk_ref[...], g) for g in groups]
    kk_sq = [head_sum(kk[g] * kk[g]) for g in groups]
    cum = []
    for g in groups:
        dh, dl = _split_hi_lo(logd[g])
        cum.append(_dot(tril_ones, dh) + _dot(tril_ones, dl))
    k2 = [k[g] * (1.0 + (alpha[g] - 1.0) * cols(ka_ref[...], g)) for g in groups]
    bonus = [head_sum(r[g] * k2[g] * cols(rk_ref[...], g)) * v[g] for g in groups]
    kkn = [kk[g] / jnp.maximum(jnp.sqrt(kk_sq[g]), 1e-12) for g in groups]
    b_all = [kkn[g] * alpha[g] for g in groups]
    at_all = [-kkn[g] * jnp.exp(cum[g] - logd[g]) for g in groups]
    rt_all = [r[g] * jnp.exp(cum[g]) for g in groups]
    inv_all = [jnp.exp(-cum[g]) for g in groups]
    bt_all = [b_all[g] * inv_all[g] for g in groups]
    kt_all = [k2[g] * inv_all[g] for g in groups]

    x, ab, ak = {}, {}, {}
    for (ci, g) in units:
        x[ci, g] = jnp.concatenate([chunk(at_all[g], ci), chunk(rt_all[g], ci)], axis=0)
        ab[ci, g] = _mm(x[ci, g], bd(chunk(bt_all[g], ci)), _NT)
        ak[ci, g] = _mm(x[ci, g], bd(chunk(kt_all[g], ci)), _NT)
    a_ab = {u_: jnp.where(strict, ab[u_][:c], 0.0) for u_ in units}
    a_rb = {u_: jnp.where(incl, ab[u_][c:], 0.0) for u_ in units}
    a_ak = {u_: jnp.where(strict, ak[u_][:c], 0.0) for u_ in units}
    a_rk = {u_: jnp.where(incl, ak[u_][c:], 0.0) for u_ in units}

    blk_t, blk_s = t2 >> 3, s2 >> 3
    a_dg = {u_: jnp.where(blk_t == blk_s, a_ab[u_], 0.0) for u_ in units}
    t_inv = {u_: eye4 + a_dg[u_] for u_ in units}
    pw = {u_: _mm(a_dg[u_], bd(a_dg[u_])) for u_ in units}
    av = {(ci, g): _mm(jnp.concatenate([a_ak[ci, g], a_rk[ci, g]], axis=0), bd(chunk(v[g], ci)))
          for (ci, g) in units}
    tp = {u_: _mm(jnp.concatenate([t_inv[u_], pw[u_]], axis=0), bd(pw[u_])) for u_ in units}
    t_inv = {u_: t_inv[u_] + tp[u_][:c] for u_ in units}
    t_inv = {u_: t_inv[u_] + _mm(t_inv[u_], bd(tp[u_][c:])) for u_ in units}
    for shift in range(RWKV_INV_BASE_LOG2, RWKV_CHUNK_LOG2):
        couple = ((t2 >> (shift + 1)) == (s2 >> (shift + 1))) & ((t2 >> shift) != (s2 >> shift))
        de = {u_: _mm(t_inv[u_], bd(jnp.where(couple, a_ab[u_], 0.0))) for u_ in units}
        t_inv = {u_: t_inv[u_] + _mm(de[u_], bd(t_inv[u_])) for u_ in units}

    state = [s_ref[g] for g in groups]
    y = {}
    for ci in range(RWKV_STEP_CHUNKS):
        xs = [_mm(x[ci, g], state[g], _NT) for g in groups]
        u = [_mm(t_inv[ci, g], bd(xs[g][:c] + av[ci, g][:c])) for g in groups]
        for g in groups:
            y[ci, g] = xs[g][c:] + _mm(a_rb[ci, g], bd(u[g])) + av[ci, g][c:]
        for g in groups:
            cum_end = chunk(cum[g], ci)[c - 1:c, :]
            to_end = jnp.exp(cum_end - chunk(cum[g], ci))
            b_end = chunk(b_all[g], ci) * to_end
            k_end = chunk(k2[g], ci) * to_end
            upd = _mm(jnp.concatenate([u[g], chunk(v[g], ci)], axis=0),
                      jnp.concatenate([b_end, k_end], axis=0), _TN)
            state[g] = state[g] * jnp.exp(cum_end) + jnp.where(bdmask, upd, 0.0)
    for g in groups:
        s_ref[g] = state[g]

    yf = [jnp.concatenate([y[ci, g] for ci in range(RWKV_STEP_CHUNKS)], axis=0) for g in groups]
    mean = [head_sum(yf[g]) * (1.0 / HEAD_DIM) for g in groups]
    yc = [yf[g] - mean[g] for g in groups]
    var = [head_sum(yc[g] * yc[g]) * (1.0 / HEAD_DIM) for g in groups]
    for g in groups:
        yn = yc[g] * lax.rsqrt(var[g] + GN_EPS) * cols(lng_ref[...], g) + cols(lnb_ref[...], g)
        o_ref[:, g * gw:(g + 1) * gw] = ((yn + bonus[g]) * cols(gate_all, g)).astype(o_ref.dtype)


def _rwkv_mix(p_r, mu, w0, wdec, a0, wa, wg, k_k, k_a, r_k, lnx_g, lnx_b):
    s = p_r.shape[0]
    rows = RWKV_STEP_ROWS
    row = lambda n: pl.BlockSpec((1, n), lambda i: (0, 0))
    full = lambda a: pl.BlockSpec(a.shape, lambda i: (0, 0))
    return pl.pallas_call(
        _rwkv_kernel,
        grid=(s // rows,),
        in_specs=[
            pl.BlockSpec((rows, RWKV_COLS), lambda i: (i, 0)),
            row(RWKV_COLS), row(RWKV_WIDTH), full(wdec), row(RWKV_WIDTH), full(wa), full(wg),
            row(RWKV_WIDTH), row(RWKV_WIDTH), row(RWKV_WIDTH), row(RWKV_WIDTH), row(RWKV_WIDTH),
        ],
        out_specs=pl.BlockSpec((rows, RWKV_WIDTH), lambda i: (i, 0)),
        out_shape=jax.ShapeDtypeStruct((s, RWKV_WIDTH), BF16),
        scratch_shapes=[
            pltpu.VMEM((1, RWKV_COLS), F32),
            pltpu.VMEM((RWKV_GROUPS, RWKV_GROUP_WIDTH, RWKV_GROUP_WIDTH), F32),
        ],
        compiler_params=pltpu.CompilerParams(
            dimension_semantics=("arbitrary",), vmem_limit_bytes=V7X_VMEM_LIMIT),
        name="rwkv7_mix",
    )(p_r, mu, w0, wdec, a0, wa, wg, k_k, k_a, r_k, lnx_g, lnx_b)


def _swa_kernel(sink_ref, q_ref, kc_ref, vc_ref, kp_ref, vp_ref, g_ref, o_ref):
    w, hd = WINDOW, HEAD_DIM
    kvs = range(ATTN_KV_HEADS)
    blk = pl.program_id(0)
    qi = lax.broadcasted_iota(jnp.int32, (w, 2 * w), 0)
    kj = lax.broadcasted_iota(jnp.int32, (w, 2 * w), 1)
    mask = (kj > qi) & (kj <= qi + w) & ((kj >= w) | (blk > 0))
    neg = jnp.where(mask, 0.0, -jnp.inf)
    neg = jnp.concatenate([neg] * ATTN_GROUP, axis=0)
    grp_row = lax.broadcasted_iota(jnp.int32, (ATTN_GROUP * w, 1), 0) >> 7
    q = q_ref[...] * (hd ** -0.5)

    heads = [[kv * ATTN_GROUP + j for j in range(ATTN_GROUP)] for kv in kvs]
    s, vb, sink = [], [], []
    for kv in kvs:
        ksl = slice(kv * hd, (kv + 1) * hd)
        kb = jnp.concatenate([kp_ref[:, ksl], kc_ref[:, ksl]], axis=0)
        vb.append(jnp.concatenate([vp_ref[:, ksl], vc_ref[:, ksl]], axis=0))
        qs = jnp.concatenate([q[:, h * hd:(h + 1) * hd] for h in heads[kv]], axis=0)
        s.append(_dot(qs, kb, _NT))
        sk = jnp.full((ATTN_GROUP * w, 1), sink_ref[heads[kv][0]], F32)
        for j in range(1, ATTN_GROUP):
            sk = jnp.where(grp_row == j, sink_ref[heads[kv][j]], sk)
        sink.append(sk)
    e, denom = [], []
    for kv in kvs:
        sm = s[kv] + neg
        m = jnp.maximum(jnp.max(sm, axis=-1, keepdims=True), sink[kv])
        ex = jnp.exp(sm - m)
        denom.append(jnp.sum(ex, axis=-1, keepdims=True) + jnp.exp(sink[kv] - m))
        e.append(ex.astype(BF16))
    outs = [None] * ATTN_Q_HEADS
    for kv in kvs:
        o = _dot(e[kv], vb[kv]) / denom[kv]
        for j, h in enumerate(heads[kv]):
            outs[h] = o[j * w:(j + 1) * w, :]
    y = jnp.concatenate(outs, axis=-1)
    o_ref[...] = _rmsnorm(y, g_ref[...]).astype(o_ref.dtype)


def _swa_mix(p_a, sinks, norm_g):
    s = p_a.shape[0]
    w = WINDOW
    kcol = ATTN_WIDTH // ATTN_KV_WIDTH
    return pl.pallas_call(
        _swa_kernel,
        grid=(s // w,),
        in_specs=[
            pl.BlockSpec(memory_space=pltpu.SMEM),
            pl.BlockSpec((w, ATTN_WIDTH), lambda i: (i, 0)),
            pl.BlockSpec((w, ATTN_KV_WIDTH), lambda i: (i, kcol)),
            pl.BlockSpec((w, ATTN_KV_WIDTH), lambda i: (i, kcol + 1)),
            pl.BlockSpec((w, ATTN_KV_WIDTH), lambda i: (jnp.maximum(i - 1, 0), kcol)),
            pl.BlockSpec((w, ATTN_KV_WIDTH), lambda i: (jnp.maximum(i - 1, 0), kcol + 1)),
            pl.BlockSpec((1, ATTN_WIDTH), lambda i: (0, 0)),
        ],
        out_specs=pl.BlockSpec((w, ATTN_WIDTH), lambda i: (i, 0)),
        out_shape=jax.ShapeDtypeStruct((s, ATTN_WIDTH), BF16),
        compiler_params=pltpu.CompilerParams(dimension_semantics=("parallel",)),
        name="swa_sink_mix",
    )(sinks, p_a, p_a, p_a, p_a, p_a, norm_g)


def _gmlp_kernel(p_ref, lng_ref, lnb_ref, ws_ref, bias_ref, g_ref, o_ref):
    ch, hd = GM_CHUNK, GM_HEAD_DIM
    p = p_ref[...]
    z = 0.5 * p * (1.0 + lax.erf(p * (2.0 ** -0.5)))
    u = z[:, :GM_WIDTH]
    z2 = z[:, GM_WIDTH:]
    mu = jnp.mean(z2, axis=-1, keepdims=True)
    zc = z2 - mu
    var = jnp.mean(zc * zc, axis=-1, keepdims=True)
    z2 = (zc * lax.rsqrt(var + NORM_EPS) * lng_ref[...] + lnb_ref[...]).astype(BF16)
    ti = lax.broadcasted_iota(jnp.int32, (ch, ch), 0)
    si = lax.broadcasted_iota(jnp.int32, (ch, ch), 1)
    causal = si <= ti
    wh = [jnp.where(causal, ws_ref[h], 0.0).astype(BF16) for h in range(GM_HEADS)]
    bias = bias_ref[...]
    for ci in range(GM_STEP_CHUNKS):
        rows = slice(ci * ch, (ci + 1) * ch)
        mixed = [_dot(wh[h], z2[rows, h * hd:(h + 1) * hd]) for h in range(GM_HEADS)]
        y = u[rows, :] * (jnp.concatenate(mixed, axis=-1) + bias)
        o_ref[rows, :] = _rmsnorm(y, g_ref[...]).astype(o_ref.dtype)


def _gmlp_mix(p_g, ln_g, ln_b, ws, bias_full, norm_g):
    s = p_g.shape[0]
    ch = GM_CHUNK
    rows = GM_STEP_CHUNKS * ch
    row = pl.BlockSpec((1, GM_WIDTH), lambda i: (0, 0))
    return pl.pallas_call(
        _gmlp_kernel,
        grid=(s // rows,),
        in_specs=[
            pl.BlockSpec((rows, GM_COLS), lambda i: (i, 0)),
            row, row,
            pl.BlockSpec((GM_HEADS, ch, ch), lambda i: (0, 0, 0)),
            pl.BlockSpec((ch, GM_WIDTH), lambda i: (0, 0)),
            row,
        ],
        out_specs=pl.BlockSpec((rows, GM_WIDTH), lambda i: (i, 0)),
        out_shape=jax.ShapeDtypeStruct((s, GM_WIDTH), BF16),
        compiler_params=pltpu.CompilerParams(dimension_semantics=("parallel",)),
        name="chunk_gmlp_mix",
    )(p_g, ln_g, ln_b, ws, bias_full, norm_g)


def _out_proj_kernel(yr_ref, ya_ref, yg_ref, wr_ref, wa_ref, wg_ref, x_ref, o_ref):
    acc = (_dot(yr_ref[...], wr_ref[...].astype(BF16)) + _dot(ya_ref[...], wa_ref[...].astype(BF16))
           + _dot(yg_ref[...], wg_ref[...].astype(BF16)))
    o_ref[...] = x_ref[...] + acc


def _out_proj(y_r, y_a, y_g, w, layer, x, *, tm, tn):
    m, d = x.shape
    lhs = lambda n: pl.BlockSpec((tm, n), lambda i, j: (i, 0))
    rhs = lambda n, blk: pl.BlockSpec((None, n, tn), lambda i, j: (layer, blk, j))
    return pl.pallas_call(
        _out_proj_kernel,
        grid=(m // tm, d // tn),
        in_specs=[lhs(RWKV_WIDTH), lhs(ATTN_WIDTH), lhs(GM_WIDTH),
                  rhs(RWKV_WIDTH, 0), rhs(ATTN_WIDTH, 1), rhs(GM_WIDTH, (RWKV_WIDTH + ATTN_WIDTH) // GM_WIDTH),
                  pl.BlockSpec((tm, tn), lambda i, j: (i, j))],
        out_specs=pl.BlockSpec((tm, tn), lambda i, j: (i, j)),
        out_shape=jax.ShapeDtypeStruct((m, d), F32),
        compiler_params=pltpu.CompilerParams(
            dimension_semantics=("parallel", "parallel"), vmem_limit_bytes=V7X_VMEM_LIMIT),
        name="out_proj_residual",
    )(y_r, y_a, y_g, w, w, w, x)


def _ffn_kernel(x_ref, g_ref, up_ref, down_ref, gf_ref, o_ref, h_ref, *, final_norm):
    @pl.when(pl.program_id(1) == 0)
    def _():
        x = x_ref[...]
        h_ref[...] = _rmsnorm(x, g_ref[...]).astype(BF16)
        o_ref[...] = x

    a = jnp.maximum(_dot(h_ref[...], up_ref[...].astype(BF16)), 0.0)
    o_ref[...] += _dot((a * a).astype(BF16), down_ref[...].astype(BF16))

    if final_norm:
        @pl.when(pl.program_id(1) == pl.num_programs(1) - 1)
        def _():
            o_ref[...] = _rmsnorm(o_ref[...], gf_ref[...])


def _ffn(x, g, w_up, w_down, layer, g_final, *, final_norm, tm, tf):
    m, d = x.shape
    f = w_up.shape[2]
    return pl.pallas_call(
        functools.partial(_ffn_kernel, final_norm=final_norm),
        grid=(m // tm, f // tf),
        in_specs=[
            pl.BlockSpec((tm, d), lambda i, j: (i, 0)),
            pl.BlockSpec((1, d), lambda i, j: (0, 0)),
            pl.BlockSpec((None, d, tf), lambda i, j: (layer, 0, j)),
            pl.BlockSpec((None, tf, d), lambda i, j: (layer, j, 0)),
            pl.BlockSpec((1, d), lambda i, j: (0, 0)),
        ],
        out_specs=pl.BlockSpec((tm, d), lambda i, j: (i, 0)),
        out_shape=jax.ShapeDtypeStruct((m, d), F32),
        scratch_shapes=[pltpu.VMEM((tm, d), BF16)],
        compiler_params=pltpu.CompilerParams(
            dimension_semantics=("parallel", "arbitrary"), vmem_limit_bytes=V7X_VMEM_LIMIT),
        name="relu2_mlp_residual",
    )(x, g, w_up, w_down, g_final)


def kernel(x, ln1_g, w_in, rwkv_mu, rwkv_w0, rwkv_decay_up, rwkv_a0, rwkv_a_up, rwkv_g_up, rwkv_k_k,
           rwkv_k_a, rwkv_r_k, rwkv_lnx_g, rwkv_lnx_b, attn_sinks, attn_norm_g, gm_ln_g, gm_ln_b, gm_ws,
           gm_bs, gm_norm_g, w_out, ln2_g, w_ffn_up, w_ffn_down, lnf_g):
    batch, seq, d = x.shape
    assert (batch, seq, d) == (1, SEQ, D_MODEL)
    xs = x.reshape(seq, d)
    row = lambda a: a.reshape(1, -1)
    zeros_lora = jnp.zeros((LORA_IN - DECAY_LORA, RWKV_WIDTH), F32)
    gm_start = RWKV_COLS + ATTN_COLS
    for l in range(DEPTH):
        g1 = row(ln1_g[l])
        p_r = _norm_matmul(xs, g1, w_in, l, col_block=0, n=RWKV_COLS, tm=1024, tn=640,
                           out_dtype=F32, name="norm_in_proj_rwkv")
        p_a = _norm_matmul(xs, g1, w_in, l, col_block=RWKV_COLS // 640, n=ATTN_COLS, tm=1024, tn=640,
                           out_dtype=BF16, name="norm_in_proj_attn")
        p_g = _norm_matmul(xs, g1, w_in, l, col_block=gm_start // 256, n=GM_COLS, tm=1024, tn=256,
                           out_dtype=F32, name="norm_in_proj_gmlp")

        wdec = jnp.concatenate([rwkv_decay_up[l], zeros_lora], axis=0).astype(BF16)
        wa = jnp.concatenate([zeros_lora, rwkv_a_up[l]], axis=0).astype(BF16)
        y_r = _rwkv_mix(p_r, row(rwkv_mu[l]), row(rwkv_w0[l]), wdec, row(rwkv_a0[l]), wa,
                        rwkv_g_up[l].astype(BF16), row(rwkv_k_k[l]), row(rwkv_k_a[l]), row(rwkv_r_k[l]),
                        row(rwkv_lnx_g[l]), row(rwkv_lnx_b[l]))
        y_a = _swa_mix(p_a, attn_sinks[l], row(attn_norm_g[l]))
        bias_full = jnp.repeat(gm_bs[l].T, GM_HEAD_DIM, axis=1)
        y_g = _gmlp_mix(p_g, row(gm_ln_g[l]), row(gm_ln_b[l]), gm_ws[l], bias_full, row(gm_norm_g[l]))

        xs = _out_proj(y_r, y_a, y_g, w_out, l, xs, tm=1024, tn=1024)
        xs = _ffn(xs, row(ln2_g[l]), w_ffn_up, w_ffn_down, l, row(lnf_g), final_norm=(l == DEPTH - 1),
                  tm=1024, tf=512)
    return xs.reshape(batch, seq, d)
```

```python
import functools

import jax
import jax.numpy as jnp
from jax import lax
from jax.experimental import pallas as pl
from jax.experimental.pallas import tpu as pltpu

F32 = jnp.float32
BF16 = jnp.bfloat16

D_MODEL = 2048
SEQ = 8192
DEPTH = 4
HEAD_DIM = 64
RWKV_HEADS = 12
RWKV_WIDTH = RWKV_HEADS * HEAD_DIM
DECAY_LORA = 64
ICLR_LORA = 64
GATE_LORA = 128
LORA_IN = DECAY_LORA + ICLR_LORA
ATTN_Q_HEADS = 12
ATTN_KV_HEADS = 4
ATTN_GROUP = ATTN_Q_HEADS // ATTN_KV_HEADS
ATTN_WIDTH = ATTN_Q_HEADS * HEAD_DIM
ATTN_KV_WIDTH = ATTN_KV_HEADS * HEAD_DIM
WINDOW = 128
GM_HEADS = 4
GM_HEAD_DIM = 128
GM_WIDTH = GM_HEADS * GM_HEAD_DIM
GM_CHUNK = 128
GM_STEP_CHUNKS = 4
MIX_WIDTH = RWKV_WIDTH + ATTN_WIDTH + GM_WIDTH
D_FF = 4 * D_MODEL
RWKV_COLS = 3 * RWKV_WIDTH + DECAY_LORA + ICLR_LORA + GATE_LORA
ATTN_COLS = ATTN_WIDTH + 2 * ATTN_KV_WIDTH
GM_COLS = 2 * GM_WIDTH
NORM_EPS = 1e-6
GN_EPS = 64e-5

RWKV_CHUNK = 64
RWKV_GROUP_HEADS = 4
RWKV_GROUP_WIDTH = RWKV_GROUP_HEADS * HEAD_DIM
RWKV_GROUPS = RWKV_HEADS // RWKV_GROUP_HEADS
RWKV_PAIR_CHUNKS = 2
RWKV_PAIR_ROWS = RWKV_PAIR_CHUNKS * RWKV_CHUNK
RWKV_STEP_PAIRS = 4
RWKV_STEP_ROWS = RWKV_STEP_PAIRS * RWKV_PAIR_ROWS
RWKV_CHUNK_LOG2 = 6
RWKV_INV_BASE_LOG2 = 3

V7X_VMEM_LIMIT = 56 * 1024 * 1024

_NN = (((1,), (0,)), ((), ()))
_NT = (((1,), (1,)), ((), ()))
_TN = (((0,), (0,)), ((), ()))


def _dot(a, b, dims=_NN):
    return lax.dot_general(a, b, dims, preferred_element_type=F32)


def _split_hi_lo(x):
    hi = x.astype(BF16)
    lo = (x - hi.astype(F32)).astype(BF16)
    return hi, lo


def _mm(a, b, dims=_NN):
    return _dot(a.astype(BF16), b.astype(BF16), dims)


def _rmsnorm(x, g):
    return x * lax.rsqrt(jnp.mean(x * x, axis=-1, keepdims=True) + NORM_EPS) * g


def _alternate(*iterators):
    live = list(iterators)
    while live:
        for it in list(live):
            try:
                next(it)
            except StopIteration:
                live.remove(it)


def _norm_matmul_kernel(x_ref, g_ref, w_ref, o_ref):
    h = _rmsnorm(x_ref[...], g_ref[...]).astype(BF16)
    o_ref[...] = _dot(h, w_ref[...]).astype(o_ref.dtype)


def _norm_matmul(x, g, w, layer, *, col_start, n, tm, tn, out_dtype, name):
    m, d = x.shape
    return pl.pallas_call(
        _norm_matmul_kernel,
        grid=(m // tm, n // tn),
        in_specs=[
            pl.BlockSpec((tm, d), lambda i, j: (i, 0)),
            pl.BlockSpec((1, d), lambda i, j: (0, 0)),
            pl.BlockSpec((pl.Element(d), pl.Element(tn)),
                         lambda i, j: (layer * d, pl.multiple_of(col_start + j * tn, 128))),
        ],
        out_specs=pl.BlockSpec((tm, tn), lambda i, j: (i, j)),
        out_shape=jax.ShapeDtypeStruct((m, n), out_dtype),
        compiler_params=pltpu.CompilerParams(
            dimension_semantics=("parallel", "parallel"), vmem_limit_bytes=V7X_VMEM_LIMIT),
        name=name,
    )(x, g, w)


def _rwkv_kernel(p_ref, mu_ref, w0_ref, wdec_ref, a0_ref, wa_ref, wg_ref, kk_ref, ka_ref, rk_ref,
                 lng_ref, lnb_ref, o_ref, prev_ref, s_ref):
    c, gw, prows = RWKV_CHUNK, RWKV_GROUP_WIDTH, RWKV_PAIR_ROWS
    w = RWKV_WIDTH
    groups = range(RWKV_GROUPS)
    chunks = range(RWKV_PAIR_CHUNKS)
    units = [(ci, g) for ci in chunks for g in groups]

    @pl.when(pl.program_id(0) == 0)
    def _():
        prev_ref[...] = jnp.zeros_like(prev_ref)
        s_ref[...] = jnp.zeros_like(s_ref)

    rowh = lax.broadcasted_iota(jnp.int32, (gw, gw), 0) >> 6
    colh = lax.broadcasted_iota(jnp.int32, (gw, gw), 1) >> 6
    bdmask = rowh == colh
    head_ones = bdmask.astype(BF16)
    ti = lax.broadcasted_iota(jnp.int32, (prows, prows), 0)
    si = lax.broadcasted_iota(jnp.int32, (prows, prows), 1)
    tril_ones = ((si <= ti) & ((si >> 6) == (ti >> 6))).astype(BF16)
    t2 = lax.broadcasted_iota(jnp.int32, (c, gw), 0)
    s2 = lax.broadcasted_iota(jnp.int32, (c, gw), 1) & (c - 1)
    strict = s2 < t2
    incl = s2 <= t2
    eye4 = (s2 == t2).astype(F32)
    lane = lax.broadcasted_iota(jnp.int32, (c, 2 * HEAD_DIM), 1)
    even_head = (lane < HEAD_DIM).astype(BF16)
    odd_head = (lane >= HEAD_DIM).astype(BF16)
    zero_slab = jnp.zeros((c, 2 * HEAD_DIM), BF16)
    first_row = lax.broadcasted_iota(jnp.int32, (8, RWKV_COLS), 0) == 0

    def head_sum(x):
        return _dot(x.astype(BF16), head_ones)

    def bd(x):
        x = x.astype(BF16)
        lo, hi = x[:, :2 * HEAD_DIM], x[:, 2 * HEAD_DIM:]
        return jnp.concatenate([
            jnp.concatenate([lo * even_head, zero_slab], axis=1),
            jnp.concatenate([lo * odd_head, zero_slab], axis=1),
            jnp.concatenate([zero_slab, hi * even_head], axis=1),
            jnp.concatenate([zero_slab, hi * odd_head], axis=1)], axis=0)

    def cols(x, g):
        return x[:, g * gw:(g + 1) * gw]

    def chunk(x, ci):
        return x[ci * c:(ci + 1) * c, :]

    state = [s_ref[g] for g in groups]
    ops = {}

    def operands(pi):
        r0 = pi * prows
        p = p_ref[r0:r0 + prows, :]
        before = prev_ref[...] if pi == 0 else p_ref[r0 - 1:r0, :]
        rolled = pltpu.roll(p, 1, axis=0)
        shifted = jnp.concatenate([jnp.where(first_row, before, rolled[:8]), rolled[8:]], axis=0)
        p = p + (shifted - p) * mu_ref[...]
        yield
        x_lora = p[:, 3 * w:3 * w + LORA_IN]
        x_gate = p[:, 3 * w + LORA_IN:]
        dec = -(jax.nn.softplus(-(w0_ref[...] + _mm(jnp.tanh(x_lora), wdec_ref[...])))) - 0.5
        logd_all = -jnp.exp(dec)
        alpha_all = jax.nn.sigmoid(a0_ref[...] + _mm(x_lora, wa_ref[...]))
        o = ops[pi] = {"gate": _mm(jax.nn.sigmoid(x_gate), wg_ref[...])}
        for name in ("at", "rt", "bt", "kt", "v", "wend", "dend", "bonus"):
            o[name] = {}
        yield
        for g in groups:
            r, k, v = cols(p[:, 0:w], g), cols(p[:, w:2 * w], g), cols(p[:, 2 * w:3 * w], g)
            alpha, logd = cols(alpha_all, g), cols(logd_all, g)
            kk = k * cols(kk_ref[...], g)
            kk_sq = head_sum(kk * kk)
            dh, dl = _split_hi_lo(logd)
            cum = _dot(tril_ones, dh) + _dot(tril_ones, dl)
            k2 = k * (1.0 + (alpha - 1.0) * cols(ka_ref[...], g))
            o["bonus"][g] = head_sum(r * k2 * cols(rk_ref[...], g)) * v
            o["v"][g] = v.astype(BF16)
            yield
            kkn = kk / jnp.maximum(jnp.sqrt(kk_sq), 1e-12)
            b = kkn * alpha
            inv = jnp.exp(-cum)
            o["at"][g] = (-kkn * jnp.exp(cum - logd)).astype(BF16)
            o["rt"][g] = (r * jnp.exp(cum)).astype(BF16)
            yield
            o["bt"][g] = (b * inv).astype(BF16)
            o["kt"][g] = (k2 * inv).astype(BF16)
            yield
            for ci in chunks:
                cum_end = chunk(cum, ci)[c - 1:c, :]
                to_end = jnp.exp(cum_end - chunk(cum, ci))
                o["wend"][ci, g] = jnp.concatenate(
                    [(chunk(b, ci) * to_end).astype(BF16), (chunk(k2, ci) * to_end).astype(BF16)], axis=0)
                o["dend"][ci, g] = jnp.exp(cum_end)
            yield

    def chain(pi):
        o = ops[pi]
        x, ab, ak, vq = {}, {}, {}, {}
        for (ci, g) in units:
            x[ci, g] = jnp.concatenate([chunk(o["at"][g], ci), chunk(o["rt"][g], ci)], axis=0)
            vq[ci, g] = chunk(o["v"][g], ci)
            ab[ci, g] = _dot(x[ci, g], bd(chunk(o["bt"][g], ci)), _NT)
            ak[ci, g] = _dot(x[ci, g], bd(chunk(o["kt"][g], ci)), _NT)
        yield
        a_ab = {u_: jnp.where(strict, ab[u_][:c], 0.0) for u_ in units}
        a_rb = {u_: jnp.where(incl, ab[u_][c:], 0.0) for u_ in units}
        a_ak = {u_: jnp.where(strict, ak[u_][:c], 0.0) for u_ in units}
        a_rk = {u_: jnp.where(incl, ak[u_][c:], 0.0) for u_ in units}

        blk_t, blk_s = t2 >> 3, s2 >> 3
        a_dg = {u_: jnp.where(blk_t == blk_s, a_ab[u_], 0.0) for u_ in units}
        t_inv = {u_: eye4 + a_dg[u_] for u_ in units}
        pw = {u_: _mm(a_dg[u_], bd(a_dg[u_])) for u_ in units}
        av = {u_: _mm(jnp.concatenate([a_ak[u_], a_rk[u_]], axis=0), bd(vq[u_])) for u_ in units}
        yield
        tp = {u_: _mm(jnp.concatenate([t_inv[u_], pw[u_]], axis=0), bd(pw[u_])) for u_ in units}
        yield
        t_inv = {u_: t_inv[u_] + tp[u_][:c] for u_ in units}
        t_inv = {u_: t_inv[u_] + _mm(t_inv[u_], bd(tp[u_][c:])) for u_ in units}
        yield
        for shift in range(RWKV_INV_BASE_LOG2, RWKV_CHUNK_LOG2):
            couple = ((t2 >> (shift + 1)) == (s2 >> (shift + 1))) & ((t2 >> shift) != (s2 >> shift))
            de = {u_: _mm(t_inv[u_], bd(jnp.where(couple, a_ab[u_], 0.0))) for u_ in units}
            yield
            t_inv = {u_: t_inv[u_] + _mm(de[u_], bd(t_inv[u_])) for u_ in units}
            yield

        y = {}
        for ci in chunks:
            xs = [_mm(x[ci, g], state[g], _NT) for g in groups]
            yield
            u = [_mm(t_inv[ci, g], bd(xs[g][:c] + av[ci, g][:c])) for g in groups]
            yield
            for g in groups:
                y[ci, g] = xs[g][c:] + _mm(a_rb[ci, g], bd(u[g])) + av[ci, g][c:]
            for g in groups:
                upd = _dot(jnp.concatenate([u[g].astype(BF16), vq[ci, g]], axis=0), o["wend"][ci, g], _TN)
                state[g] = state[g] * o["dend"][ci, g] + jnp.where(bdmask, upd, 0.0)
            yield

        yf = [jnp.concatenate([y[ci, g] for ci in chunks], axis=0) for g in groups]
        mean = [head_sum(yf[g]) * (1.0 / HEAD_DIM) for g in groups]
        yield
        yc = [yf[g] - mean[g] for g in groups]
        var = [head_sum(yc[g] * yc[g]) * (1.0 / HEAD_DIM) for g in groups]
        yield
        r0 = pi * prows
        for g in groups:
            yn = yc[g] * lax.rsqrt(var[g] + GN_EPS) * cols(lng_ref[...], g) + cols(lnb_ref[...], g)
            o_ref[r0:r0 + prows, g * gw:(g + 1) * gw] = (
                (yn + o["bonus"][g]) * cols(o["gate"], g)).astype(o_ref.dtype)
        del ops[pi]

    _alternate(operands(0))
    for pi in range(RWKV_STEP_PAIRS):
        if pi + 1 < RWKV_STEP_PAIRS:
            _alternate(chain(pi), operands(pi + 1))
        else:
            _alternate(chain(pi))
    prev_ref[...] = p_ref[RWKV_STEP_ROWS - 1:RWKV_STEP_ROWS, :]
    for g in groups:
        s_ref[g] = state[g]


def _rwkv_mix(p_r, mu, w0, wdec, a0, wa, wg, k_k, k_a, r_k, lnx_g, lnx_b):
    s = p_r.shape[0]
    rows = RWKV_STEP_ROWS
    row = lambda n: pl.BlockSpec((1, n), lambda i: (0, 0))
    full = lambda a: pl.BlockSpec(a.shape, lambda i: (0, 0))
    return pl.pallas_call(
        _rwkv_kernel,
        grid=(s // rows,),
        in_specs=[
            pl.BlockSpec((rows, RWKV_COLS), lambda i: (i, 0)),
            row(RWKV_COLS), row(RWKV_WIDTH), full(wdec), row(RWKV_WIDTH), full(wa), full(wg),
            row(RWKV_WIDTH), row(RWKV_WIDTH), row(RWKV_WIDTH), row(RWKV_WIDTH), row(RWKV_WIDTH),
        ],
        out_specs=pl.BlockSpec((rows, RWKV_WIDTH), lambda i: (i, 0)),
        out_shape=jax.ShapeDtypeStruct((s, RWKV_WIDTH), BF16),
        scratch_shapes=[
            pltpu.VMEM((1, RWKV_COLS), F32),
            pltpu.VMEM((RWKV_GROUPS, RWKV_GROUP_WIDTH, RWKV_GROUP_WIDTH), F32),
        ],
        compiler_params=pltpu.CompilerParams(
            dimension_semantics=("arbitrary",), vmem_limit_bytes=V7X_VMEM_LIMIT),
        name="rwkv7_mix",
    )(p_r, mu, w0, wdec, a0, wa, wg, k_k, k_a, r_k, lnx_g, lnx_b)


def _swa_kernel(sink_ref, q_ref, kc_ref, vc_ref, kp_ref, vp_ref, g_ref, o_ref):
    w, hd = WINDOW, HEAD_DIM
    kvs = range(ATTN_KV_HEADS)
    blk = pl.program_id(0)
    qi = lax.broadcasted_iota(jnp.int32, (w, 2 * w), 0)
    kj = lax.broadcasted_iota(jnp.int32, (w, 2 * w), 1)
    mask = (kj > qi) & (kj <= qi + w) & ((kj >= w) | (blk > 0))
    neg = jnp.where(mask, 0.0, -jnp.inf)
    neg = jnp.concatenate([neg] * ATTN_GROUP, axis=0)
    grp_row = lax.broadcasted_iota(jnp.int32, (ATTN_GROUP * w, 1), 0) >> 7
    q = q_ref[...] * (hd ** -0.5)

    heads = [[kv * ATTN_GROUP + j for j in range(ATTN_GROUP)] for kv in kvs]
    s, vb, sink = [], [], []
    for kv in kvs:
        ksl = slice(kv * hd, (kv + 1) * hd)
        kb = jnp.concatenate([kp_ref[:, ksl], kc_ref[:, ksl]], axis=0)
        vb.append(jnp.concatenate([vp_ref[:, ksl], vc_ref[:, ksl]], axis=0))
        qs = jnp.concatenate([q[:, h * hd:(h + 1) * hd] for h in heads[kv]], axis=0)
        s.append(_dot(qs, kb, _NT))
        sk = jnp.full((ATTN_GROUP * w, 1), sink_ref[heads[kv][0]], F32)
        for j in range(1, ATTN_GROUP):
            sk = jnp.where(grp_row == j, sink_ref[heads[kv][j]], sk)
        sink.append(sk)
    e, denom = [], []
    for kv in kvs:
        sm = s[kv] + neg
        m = jnp.maximum(jnp.max(sm, axis=-1, keepdims=True), sink[kv])
        ex = jnp.exp(sm - m)
        denom.append(jnp.sum(ex, axis=-1, keepdims=True) + jnp.exp(sink[kv] - m))
        e.append(ex.astype(BF16))
    outs = [None] * ATTN_Q_HEADS
    for kv in kvs:
        o = _dot(e[kv], vb[kv]) / denom[kv]
        for j, h in enumerate(heads[kv]):
            outs[h] = o[j * w:(j + 1) * w, :]
    y = jnp.concatenate(outs, axis=-1)
    o_ref[...] = _rmsnorm(y, g_ref[...]).astype(o_ref.dtype)


def _swa_mix(p_a, sinks, norm_g):
    s = p_a.shape[0]
    w = WINDOW
    kcol = ATTN_WIDTH // ATTN_KV_WIDTH
    return pl.pallas_call(
        _swa_kernel,
        grid=(s // w,),
        in_specs=[
            pl.BlockSpec(memory_space=pltpu.SMEM),
            pl.BlockSpec((w, ATTN_WIDTH), lambda i: (i, 0)),
            pl.BlockSpec((w, ATTN_KV_WIDTH), lambda i: (i, kcol)),
            pl.BlockSpec((w, ATTN_KV_WIDTH), lambda i: (i, kcol + 1)),
            pl.BlockSpec((w, ATTN_KV_WIDTH), lambda i: (jnp.maximum(i - 1, 0), kcol)),
            pl.BlockSpec((w, ATTN_KV_WIDTH), lambda i: (jnp.maximum(i - 1, 0), kcol + 1)),
            pl.BlockSpec((1, ATTN_WIDTH), lambda i: (0, 0)),
        ],
        out_specs=pl.BlockSpec((w, ATTN_WIDTH), lambda i: (i, 0)),
        out_shape=jax.ShapeDtypeStruct((s, ATTN_WIDTH), BF16),
        compiler_params=pltpu.CompilerParams(dimension_semantics=("parallel",)),
        name="swa_sink_mix",
    )(sinks, p_a, p_a, p_a, p_a, p_a, norm_g)


def _gmlp_kernel(p_ref, lng_ref, lnb_ref, ws_ref, bias_ref, g_ref, o_ref):
    ch, hd = GM_CHUNK, GM_HEAD_DIM
    p = p_ref[...]
    z = 0.5 * p * (1.0 + lax.erf(p * (2.0 ** -0.5)))
    u = z[:, :GM_WIDTH]
    z2 = z[:, GM_WIDTH:]
    mu = jnp.mean(z2, axis=-1, keepdims=True)
    zc = z2 - mu
    var = jnp.mean(zc * zc, axis=-1, keepdims=True)
    z2 = (zc * lax.rsqrt(var + NORM_EPS) * lng_ref[...] + lnb_ref[...]).astype(BF16)
    ti = lax.broadcasted_iota(jnp.int32, (ch, ch), 0)
    si = lax.broadcasted_iota(jnp.int32, (ch, ch), 1)
    causal = si <= ti
    wh = [jnp.where(causal, ws_ref[h], 0.0).astype(BF16) for h in range(GM_HEADS)]
    bias = bias_ref[...]
    for ci in range(GM_STEP_CHUNKS):
        rows = slice(ci * ch, (ci + 1) * ch)
        mixed = [_dot(wh[h], z2[rows, h * hd:(h + 1) * hd]) for h in range(GM_HEADS)]
        y = u[rows, :] * (jnp.concatenate(mixed, axis=-1) + bias)
        o_ref[rows, :] = _rmsnorm(y, g_ref[...]).astype(o_ref.dtype)


def _gmlp_mix(p_g, ln_g, ln_b, ws, bias_full, norm_g):
    s = p_g.shape[0]
    ch = GM_CHUNK
    rows = GM_STEP_CHUNKS * ch
    row = pl.BlockSpec((1, GM_WIDTH), lambda i: (0, 0))
    return pl.pallas_call(
        _gmlp_kernel,
        grid=(s // rows,),
        in_specs=[
            pl.BlockSpec((rows, GM_COLS), lambda i: (i, 0)),
            row, row,
            pl.BlockSpec((GM_HEADS, ch, ch), lambda i: (0, 0, 0)),
            pl.BlockSpec((ch, GM_WIDTH), lambda i: (0, 0)),
            row,
        ],
        out_specs=pl.BlockSpec((rows, GM_WIDTH), lambda i: (i, 0)),
        out_shape=jax.ShapeDtypeStruct((s, GM_WIDTH), BF16),
        compiler_params=pltpu.CompilerParams(dimension_semantics=("parallel",)),
        name="chunk_gmlp_mix",
    )(p_g, ln_g, ln_b, ws, bias_full, norm_g)


def _out_proj_kernel(yr_ref, ya_ref, yg_ref, wr_ref, wa_ref, wg_ref, x_ref, o_ref):
    acc = _dot(yr_ref[...], wr_ref[...]) + _dot(ya_ref[...], wa_ref[...]) + _dot(yg_ref[...], wg_ref[...])
    o_ref[...] = x_ref[...] + acc


def _out_proj(y_r, y_a, y_g, w, layer, x, *, tm, tn):
    m, d = x.shape
    lhs = lambda n: pl.BlockSpec((tm, n), lambda i, j: (i, 0))
    rhs = lambda n, blk: pl.BlockSpec((None, n, tn), lambda i, j: (layer, blk, j))
    return pl.pallas_call(
        _out_proj_kernel,
        grid=(m // tm, d // tn),
        in_specs=[lhs(RWKV_WIDTH), lhs(ATTN_WIDTH), lhs(GM_WIDTH),
                  rhs(RWKV_WIDTH, 0), rhs(ATTN_WIDTH, 1), rhs(GM_WIDTH, (RWKV_WIDTH + ATTN_WIDTH) // GM_WIDTH),
                  pl.BlockSpec((tm, tn), lambda i, j: (i, j))],
        out_specs=pl.BlockSpec((tm, tn), lambda i, j: (i, j)),
        out_shape=jax.ShapeDtypeStruct((m, d), F32),
        compiler_params=pltpu.CompilerParams(
            dimension_semantics=("parallel", "parallel"), vmem_limit_bytes=V7X_VMEM_LIMIT),
        name="out_proj_residual",
    )(y_r, y_a, y_g, w, w, w, x)


def _ffn_kernel(x_ref, g_ref, up_ref, down_ref, gf_ref, o_ref, h_ref, *, final_norm):
    @pl.when(pl.program_id(1) == 0)
    def _():
        x = x_ref[...]
        h_ref[...] = _rmsnorm(x, g_ref[...]).astype(BF16)
        o_ref[...] = x

    a = jnp.maximum(_dot(h_ref[...], up_ref[...].astype(BF16)), 0.0)
    o_ref[...] += _dot((a * a).astype(BF16), down_ref[...].astype(BF16))

    if final_norm:
        @pl.when(pl.program_id(1) == pl.num_programs(1) - 1)
        def _():
            o_ref[...] = _rmsnorm(o_ref[...], gf_ref[...])


def _ffn(x, g, w_up, w_down, layer, g_final, *, final_norm, tm, tf):
    m, d = x.shape
    f = w_up.shape[2]
    return pl.pallas_call(
        functools.partial(_ffn_kernel, final_norm=final_norm),
        grid=(m // tm, f // tf),
        in_specs=[
            pl.BlockSpec((tm, d), lambda i, j: (i, 0)),
            pl.BlockSpec((1, d), lambda i, j: (0, 0)),
            pl.BlockSpec((None, d, tf), lambda i, j: (layer, 0, j)),
            pl.BlockSpec((None, tf, d), lambda i, j: (layer, j, 0)),
            pl.BlockSpec((1, d), lambda i, j: (0, 0)),
        ],
        out_specs=pl.BlockSpec((tm, d), lambda i, j: (i, 0)),
        out_shape=jax.ShapeDtypeStruct((m, d), F32),
        scratch_shapes=[pltpu.VMEM((tm, d), BF16)],
        compiler_params=pltpu.CompilerParams(
            dimension_semantics=("parallel", "arbitrary"), vmem_limit_bytes=V7X_VMEM_LIMIT),
        name="relu2_mlp_residual",
    )(x, g, w_up, w_down, g_final)


def kernel(x, ln1_g, w_in, rwkv_mu, rwkv_w0, rwkv_decay_up, rwkv_a0, rwkv_a_up, rwkv_g_up, rwkv_k_k,
           rwkv_k_a, rwkv_r_k, rwkv_lnx_g, rwkv_lnx_b, attn_sinks, attn_norm_g, gm_ln_g, gm_ln_b, gm_ws,
           gm_bs, gm_norm_g, w_out, ln2_g, w_ffn_up, w_ffn_down, lnf_g):
    batch, seq, d = x.shape
    assert (batch, seq, d) == (1, SEQ, D_MODEL)
    xs = x.reshape(seq, d)
    row = lambda a: a.reshape(1, -1)
    zeros_lora = jnp.zeros((LORA_IN - DECAY_LORA, RWKV_WIDTH), F32)
    gm_start = RWKV_COLS + ATTN_COLS
    w_in_b = w_in.astype(BF16).reshape(DEPTH * d, -1)
    w_out_b = w_out.astype(BF16)
    for l in range(DEPTH):
        g1 = row(ln1_g[l])
        p_r = _norm_matmul(xs, g1, w_in_b, l, col_start=0, n=RWKV_COLS, tm=1024, tn=RWKV_COLS // 2,
                           out_dtype=F32, name="norm_in_proj_rwkv")
        p_a = _norm_matmul(xs, g1, w_in_b, l, col_start=RWKV_COLS, n=ATTN_COLS, tm=1024, tn=ATTN_COLS,
                           out_dtype=BF16, name="norm_in_proj_attn")
        p_g = _norm_matmul(xs, g1, w_in_b, l, col_start=gm_start, n=GM_COLS, tm=1024, tn=GM_COLS,
                           out_dtype=F32, name="norm_in_proj_gmlp")

        wdec = jnp.concatenate([rwkv_decay_up[l], zeros_lora], axis=0).astype(BF16)
        wa = jnp.concatenate([zeros_lora, rwkv_a_up[l]], axis=0).astype(BF16)
        y_r = _rwkv_mix(p_r, row(rwkv_mu[l]), row(rwkv_w0[l]), wdec, row(rwkv_a0[l]), wa,
                        rwkv_g_up[l].astype(BF16), row(rwkv_k_k[l]), row(rwkv_k_a[l]), row(rwkv_r_k[l]),
                        row(rwkv_lnx_g[l]), row(rwkv_lnx_b[l]))
        y_a = _swa_mix(p_a, attn_sinks[l], row(attn_norm_g[l]))
        bias_full = jnp.repeat(gm_bs[l].T, GM_HEAD_DIM, axis=1)
        y_g = _gmlp_mix(p_g, row(gm_ln_g[l]), row(gm_ln_b[l]), gm_ws[l], bias_full, row(gm_norm_g[l]))

        xs = _out_proj(y_r, y_a, y_g, w_out_b, l, xs, tm=1024, tn=1024)
        xs = _ffn(xs, row(ln2_g[l]), w_ffn_up, w_ffn_down, l, row(lnf_g), final_norm=(l == DEPTH - 1),
                  tm=1024, tf=512)
    return xs.reshape(batch, seq, d)
```

```python
import functools

import jax
import jax.numpy as jnp
from jax import lax
from jax.experimental import pallas as pl
from jax.experimental.pallas import tpu as pltpu

F32 = jnp.float32
BF16 = jnp.bfloat16

D_MODEL = 2048
SEQ = 8192
DEPTH = 4
HEAD_DIM = 64
RWKV_HEADS = 12
RWKV_WIDTH = RWKV_HEADS * HEAD_DIM
DECAY_LORA = 64
ICLR_LORA = 64
GATE_LORA = 128
LORA_IN = DECAY_LORA + ICLR_LORA
ATTN_Q_HEADS = 12
ATTN_KV_HEADS = 4
ATTN_GROUP = ATTN_Q_HEADS // ATTN_KV_HEADS
ATTN_WIDTH = ATTN_Q_HEADS * HEAD_DIM
ATTN_KV_WIDTH = ATTN_KV_HEADS * HEAD_DIM
WINDOW = 128
SWA_STEP_BLOCKS = 2
GM_HEADS = 4
GM_HEAD_DIM = 128
GM_WIDTH = GM_HEADS * GM_HEAD_DIM
GM_CHUNK = 128
GM_STEP_CHUNKS = 4
MIX_WIDTH = RWKV_WIDTH + ATTN_WIDTH + GM_WIDTH
D_FF = 4 * D_MODEL
RWKV_COLS = 3 * RWKV_WIDTH + DECAY_LORA + ICLR_LORA + GATE_LORA
ATTN_COLS = ATTN_WIDTH + 2 * ATTN_KV_WIDTH
GM_COLS = 2 * GM_WIDTH
NORM_EPS = 1e-6
GN_EPS = 64e-5

RWKV_CHUNK = 64
RWKV_GROUP_HEADS = 4
RWKV_GROUP_WIDTH = RWKV_GROUP_HEADS * HEAD_DIM
RWKV_GROUPS = RWKV_HEADS // RWKV_GROUP_HEADS
RWKV_PAIR_CHUNKS = 2
RWKV_PAIR_ROWS = RWKV_PAIR_CHUNKS * RWKV_CHUNK
RWKV_STEP_PAIRS = 4
RWKV_STEP_ROWS = RWKV_STEP_PAIRS * RWKV_PAIR_ROWS
RWKV_CHUNK_LOG2 = 6
RWKV_INV_BASE_LOG2 = 3

V7X_VMEM_LIMIT = 56 * 1024 * 1024

_NN = (((1,), (0,)), ((), ()))
_NT = (((1,), (1,)), ((), ()))
_TN = (((0,), (0,)), ((), ()))


def _dot(a, b, dims=_NN):
    return lax.dot_general(a, b, dims, preferred_element_type=F32)


def _split_hi_lo(x):
    hi = x.astype(BF16)
    lo = (x - hi.astype(F32)).astype(BF16)
    return hi, lo


def _mm(a, b, dims=_NN):
    return _dot(a.astype(BF16), b.astype(BF16), dims)


def _rmsnorm(x, g):
    return x * lax.rsqrt(jnp.mean(x * x, axis=-1, keepdims=True) + NORM_EPS) * g


def _alternate(*iterators):
    live = list(iterators)
    while live:
        for it in list(live):
            try:
                next(it)
            except StopIteration:
                live.remove(it)


def _in_proj_kernel(x_ref, g_ref, w_ref, pr_ref, pa_ref, pg_ref):
    h = _rmsnorm(x_ref[...], g_ref[...]).astype(BF16)
    pr_ref[...] = _dot(h, w_ref[:, :RWKV_COLS])
    pa_ref[...] = _dot(h, w_ref[:, RWKV_COLS:RWKV_COLS + ATTN_COLS]).astype(pa_ref.dtype)
    pg_ref[...] = _dot(h, w_ref[:, RWKV_COLS + ATTN_COLS:])


def _in_proj(x, g, w, layer, *, tm):
    m, d = x.shape
    n = w.shape[2]
    out = lambda width: pl.BlockSpec((tm, width), lambda i: (i, 0))
    return pl.pallas_call(
        _in_proj_kernel,
        grid=(m // tm,),
        in_specs=[
            pl.BlockSpec((tm, d), lambda i: (i, 0)),
            pl.BlockSpec((1, d), lambda i: (0, 0)),
            pl.BlockSpec((None, d, n), lambda i: (layer, 0, 0), pipeline_mode=pl.Buffered(1)),
        ],
        out_specs=[out(RWKV_COLS), out(ATTN_COLS), out(GM_COLS)],
        out_shape=[jax.ShapeDtypeStruct((m, RWKV_COLS), F32), jax.ShapeDtypeStruct((m, ATTN_COLS), BF16),
                   jax.ShapeDtypeStruct((m, GM_COLS), F32)],
        compiler_params=pltpu.CompilerParams(
            dimension_semantics=("parallel",), vmem_limit_bytes=V7X_VMEM_LIMIT),
        name="norm_in_proj",
    )(x, g, w)


def _rwkv_kernel(p_ref, mu_ref, w0_ref, wdec_ref, a0_ref, wa_ref, wg_ref, kk_ref, ka_ref, rk_ref,
                 lng_ref, lnb_ref, o_ref, prev_ref, s_ref):
    c, gw, prows = RWKV_CHUNK, RWKV_GROUP_WIDTH, RWKV_PAIR_ROWS
    w = RWKV_WIDTH
    groups = range(RWKV_GROUPS)
    chunks = range(RWKV_PAIR_CHUNKS)
    units = [(ci, g) for ci in chunks for g in groups]

    @pl.when(pl.program_id(0) == 0)
    def _():
        prev_ref[...] = jnp.zeros_like(prev_ref)
        s_ref[...] = jnp.zeros_like(s_ref)

    rowh = lax.broadcasted_iota(jnp.int32, (gw, gw), 0) >> 6
    colh = lax.broadcasted_iota(jnp.int32, (gw, gw), 1) >> 6
    bdmask = rowh == colh
    head_ones = bdmask.astype(BF16)
    ti = lax.broadcasted_iota(jnp.int32, (prows, prows), 0)
    si = lax.broadcasted_iota(jnp.int32, (prows, prows), 1)
    tril_ones = ((si <= ti) & ((si >> 6) == (ti >> 6))).astype(BF16)
    t2 = lax.broadcasted_iota(jnp.int32, (c, gw), 0)
    s2 = lax.broadcasted_iota(jnp.int32, (c, gw), 1) & (c - 1)
    strict = s2 < t2
    incl = s2 <= t2
    eye4 = (s2 == t2).astype(F32)
    lane = lax.broadcasted_iota(jnp.int32, (c, 2 * HEAD_DIM), 1)
    even_head = (lane < HEAD_DIM).astype(BF16)
    odd_head = (lane >= HEAD_DIM).astype(BF16)
    zero_slab = jnp.zeros((c, 2 * HEAD_DIM), BF16)
    first_row = lax.broadcasted_iota(jnp.int32, (8, RWKV_COLS), 0) == 0

    def head_sum(x):
        return _dot(x.astype(BF16), head_ones)

    def bd(x):
        x = x.astype(BF16)
        lo, hi = x[:, :2 * HEAD_DIM], x[:, 2 * HEAD_DIM:]
        return jnp.concatenate([
            jnp.concatenate([lo * even_head, zero_slab], axis=1),
            jnp.concatenate([lo * odd_head, zero_slab], axis=1),
            jnp.concatenate([zero_slab, hi * even_head], axis=1),
            jnp.concatenate([zero_slab, hi * odd_head], axis=1)], axis=0)

    def cols(x, g):
        return x[:, g * gw:(g + 1) * gw]

    def chunk(x, ci):
        return x[ci * c:(ci + 1) * c, :]

    state = [s_ref[g] for g in groups]
    ops = {}

    def operands(pi):
        r0 = pi * prows
        p = p_ref[r0:r0 + prows, :]
        before = prev_ref[...] if pi == 0 else p_ref[r0 - 1:r0, :]
        rolled = pltpu.roll(p, 1, axis=0)
        shifted = jnp.concatenate([jnp.where(first_row, before, rolled[:8]), rolled[8:]], axis=0)
        p = p + (shifted - p) * mu_ref[...]
        yield
        x_lora = p[:, 3 * w:3 * w + LORA_IN]
        x_gate = p[:, 3 * w + LORA_IN:]
        dec = -(jax.nn.softplus(-(w0_ref[...] + _mm(jnp.tanh(x_lora), wdec_ref[...])))) - 0.5
        logd_all = -jnp.exp(dec)
        alpha_all = jax.nn.sigmoid(a0_ref[...] + _mm(x_lora, wa_ref[...]))
        o = ops[pi] = {"gate": _mm(jax.nn.sigmoid(x_gate), wg_ref[...])}
        for name in ("at", "rt", "bt", "kt", "v", "wend", "dend", "bonus"):
            o[name] = {}
        yield
        for g in groups:
            r, k, v = cols(p[:, 0:w], g), cols(p[:, w:2 * w], g), cols(p[:, 2 * w:3 * w], g)
            alpha, logd = cols(alpha_all, g), cols(logd_all, g)
            kk = k * cols(kk_ref[...], g)
            kk_sq = head_sum(kk * kk)
            dh, dl = _split_hi_lo(logd)
            cum = _dot(tril_ones, dh) + _dot(tril_ones, dl)
            k2 = k * (1.0 + (alpha - 1.0) * cols(ka_ref[...], g))
            o["bonus"][g] = head_sum(r * k2 * cols(rk_ref[...], g)) * v
            o["v"][g] = v.astype(BF16)
            yield
            kkn = kk / jnp.maximum(jnp.sqrt(kk_sq), 1e-12)
            b = kkn * alpha
            inv = jnp.exp(-cum)
            o["at"][g] = (-kkn * jnp.exp(cum - logd)).astype(BF16)
            o["rt"][g] = (r * jnp.exp(cum)).astype(BF16)
            yield
            o["bt"][g] = (b * inv).astype(BF16)
            o["kt"][g] = (k2 * inv).astype(BF16)
            yield
            for ci in chunks:
                cum_end = chunk(cum, ci)[c - 1:c, :]
                to_end = jnp.exp(cum_end - chunk(cum, ci))
                o["wend"][ci, g] = jnp.concatenate(
                    [(chunk(b, ci) * to_end).astype(BF16), (chunk(k2, ci) * to_end).astype(BF16)], axis=0)
                o["dend"][ci, g] = jnp.exp(cum_end)
            yield

    def head(pi):
        o = ops[pi]
        x, ab, ak, vq = {}, {}, {}, {}
        for (ci, g) in units:
            x[ci, g] = jnp.concatenate([chunk(o["at"][g], ci), chunk(o["rt"][g], ci)], axis=0)
            vq[ci, g] = chunk(o["v"][g], ci)
            ab[ci, g] = _dot(x[ci, g], bd(chunk(o["bt"][g], ci)), _NT)
            ak[ci, g] = _dot(x[ci, g], bd(chunk(o["kt"][g], ci)), _NT)
        yield
        a_ab = {u_: jnp.where(strict, ab[u_][:c], 0.0) for u_ in units}
        a_rb = {u_: jnp.where(incl, ab[u_][c:], 0.0) for u_ in units}
        a_ak = {u_: jnp.where(strict, ak[u_][:c], 0.0) for u_ in units}
        a_rk = {u_: jnp.where(incl, ak[u_][c:], 0.0) for u_ in units}

        blk_t, blk_s = t2 >> 3, s2 >> 3
        a_dg = {u_: jnp.where(blk_t == blk_s, a_ab[u_], 0.0) for u_ in units}
        t_inv = {u_: eye4 + a_dg[u_] for u_ in units}
        pw = {u_: _mm(a_dg[u_], bd(a_dg[u_])) for u_ in units}
        av = {u_: _mm(jnp.concatenate([a_ak[u_], a_rk[u_]], axis=0), bd(vq[u_])) for u_ in units}
        yield
        tp = {u_: _mm(jnp.concatenate([t_inv[u_], pw[u_]], axis=0), bd(pw[u_])) for u_ in units}
        yield
        t_inv = {u_: t_inv[u_] + tp[u_][:c] for u_ in units}
        t_inv = {u_: t_inv[u_] + _mm(t_inv[u_], bd(tp[u_][c:])) for u_ in units}
        yield
        for shift in range(RWKV_INV_BASE_LOG2, RWKV_CHUNK_LOG2):
            couple = ((t2 >> (shift + 1)) == (s2 >> (shift + 1))) & ((t2 >> shift) != (s2 >> shift))
            de = {u_: _mm(t_inv[u_], bd(jnp.where(couple, a_ab[u_], 0.0))) for u_ in units}
            yield
            t_inv = {u_: t_inv[u_] + _mm(de[u_], bd(t_inv[u_])) for u_ in units}
            yield
        o.update(x=x, vq=vq, a_rb=a_rb, av=av, t_inv=t_inv)

    def tail(pi):
        o = ops[pi]
        x, vq, a_rb, av, t_inv = o["x"], o["vq"], o["a_rb"], o["av"], o["t_inv"]
        y = {}
        for ci in chunks:
            xs = [_mm(x[ci, g], state[g], _NT) for g in groups]
            yield
            u = [_mm(t_inv[ci, g], bd(xs[g][:c] + av[ci, g][:c])) for g in groups]
            yield
            for g in groups:
                y[ci, g] = xs[g][c:] + _mm(a_rb[ci, g], bd(u[g])) + av[ci, g][c:]
            for g in groups:
                upd = _dot(jnp.concatenate([u[g].astype(BF16), vq[ci, g]], axis=0), o["wend"][ci, g], _TN)
                state[g] = state[g] * o["dend"][ci, g] + jnp.where(bdmask, upd, 0.0)
            yield

        yf = [jnp.concatenate([y[ci, g] for ci in chunks], axis=0) for g in groups]
        mean = [head_sum(yf[g]) * (1.0 / HEAD_DIM) for g in groups]
        yield
        yc = [yf[g] - mean[g] for g in groups]
        var = [head_sum(yc[g] * yc[g]) * (1.0 / HEAD_DIM) for g in groups]
        yield
        r0 = pi * prows
        for g in groups:
            yn = yc[g] * lax.rsqrt(var[g] + GN_EPS) * cols(lng_ref[...], g) + cols(lnb_ref[...], g)
            o_ref[r0:r0 + prows, g * gw:(g + 1) * gw] = (
                (yn + o["bonus"][g]) * cols(o["gate"], g)).astype(o_ref.dtype)
        del ops[pi]

    npairs = RWKV_STEP_PAIRS
    _alternate(operands(0))
    _alternate(head(0), *([operands(1)] if npairs > 1 else []))
    for pi in range(npairs):
        _alternate(tail(pi), *([head(pi + 1)] if pi + 1 < npairs else []),
                   *([operands(pi + 2)] if pi + 2 < npairs else []))
    prev_ref[...] = p_ref[RWKV_STEP_ROWS - 1:RWKV_STEP_ROWS, :]
    for g in groups:
        s_ref[g] = state[g]


def _rwkv_mix(p_r, mu, w0, wdec, a0, wa, wg, k_k, k_a, r_k, lnx_g, lnx_b):
    s = p_r.shape[0]
    rows = RWKV_STEP_ROWS
    row = lambda n: pl.BlockSpec((1, n), lambda i: (0, 0))
    full = lambda a: pl.BlockSpec(a.shape, lambda i: (0, 0))
    return pl.pallas_call(
        _rwkv_kernel,
        grid=(s // rows,),
        in_specs=[
            pl.BlockSpec((rows, RWKV_COLS), lambda i: (i, 0)),
            row(RWKV_COLS), row(RWKV_WIDTH), full(wdec), row(RWKV_WIDTH), full(wa), full(wg),
            row(RWKV_WIDTH), row(RWKV_WIDTH), row(RWKV_WIDTH), row(RWKV_WIDTH), row(RWKV_WIDTH),
        ],
        out_specs=pl.BlockSpec((rows, RWKV_WIDTH), lambda i: (i, 0)),
        out_shape=jax.ShapeDtypeStruct((s, RWKV_WIDTH), BF16),
        scratch_shapes=[
            pltpu.VMEM((1, RWKV_COLS), F32),
            pltpu.VMEM((RWKV_GROUPS, RWKV_GROUP_WIDTH, RWKV_GROUP_WIDTH), F32),
        ],
        compiler_params=pltpu.CompilerParams(
            dimension_semantics=("arbitrary",), vmem_limit_bytes=V7X_VMEM_LIMIT),
        name="rwkv7_mix",
    )(p_r, mu, w0, wdec, a0, wa, wg, k_k, k_a, r_k, lnx_g, lnx_b)


def _swa_kernel(sink_ref, q_ref, kc_ref, vc_ref, kp_ref, vp_ref, g_ref, o_ref):
    w, hd = WINDOW, HEAD_DIM
    units = [(b, kv) for b in range(SWA_STEP_BLOCKS) for kv in range(ATTN_KV_HEADS)]
    heads = [[kv * ATTN_GROUP + j for j in range(ATTN_GROUP)] for kv in range(ATTN_KV_HEADS)]
    qi = lax.broadcasted_iota(jnp.int32, (w, 2 * w), 0)
    kj = lax.broadcasted_iota(jnp.int32, (w, 2 * w), 1)
    band = (kj > qi) & (kj <= qi + w)
    first = band & ((kj >= w) | (pl.program_id(0) > 0))
    stack = lambda m: jnp.concatenate([jnp.where(m, 0.0, -jnp.inf)] * ATTN_GROUP, axis=0)
    neg = [stack(first)] + [stack(band)] * (SWA_STEP_BLOCKS - 1)
    grp_row = lax.broadcasted_iota(jnp.int32, (ATTN_GROUP * w, 1), 0) >> 7
    q = q_ref[...] * (hd ** -0.5)

    def keys(cur_ref, prev_ref, b, kv):
        ksl = slice(kv * hd, (kv + 1) * hd)
        prev = prev_ref[:, ksl] if b == 0 else cur_ref[(b - 1) * w:b * w, ksl]
        return jnp.concatenate([prev, cur_ref[b * w:(b + 1) * w, ksl]], axis=0)

    sink = []
    for kv in range(ATTN_KV_HEADS):
        sk = jnp.full((ATTN_GROUP * w, 1), sink_ref[heads[kv][0]], F32)
        for j in range(1, ATTN_GROUP):
            sk = jnp.where(grp_row == j, sink_ref[heads[kv][j]], sk)
        sink.append(sk)
    s, vb = {}, {}
    for (b, kv) in units:
        qs = jnp.concatenate([q[b * w:(b + 1) * w, h * hd:(h + 1) * hd] for h in heads[kv]], axis=0)
        s[b, kv] = _dot(qs, keys(kc_ref, kp_ref, b, kv), _NT)
        vb[b, kv] = keys(vc_ref, vp_ref, b, kv)
    e, sink_e = {}, {}
    for (b, kv) in units:
        sm = s[b, kv] + neg[b]
        m = jnp.maximum(jnp.max(sm, axis=-1, keepdims=True), sink[kv])
        e[b, kv] = jnp.exp(sm - m).astype(BF16)
        sink_e[b, kv] = jnp.exp(sink[kv] - m)
    ones = jnp.ones((2 * w, hd), BF16)
    o = {u_: _dot(e[u_], vb[u_]) / (_dot(e[u_], ones) + sink_e[u_]) for u_ in units}
    for b in range(SWA_STEP_BLOCKS):
        outs = [None] * ATTN_Q_HEADS
        for kv in range(ATTN_KV_HEADS):
            for j, h in enumerate(heads[kv]):
                outs[h] = o[b, kv][j * w:(j + 1) * w, :]
        y = jnp.concatenate(outs, axis=-1)
        o_ref[b * w:(b + 1) * w, :] = _rmsnorm(y, g_ref[...]).astype(o_ref.dtype)


def _swa_mix(p_a, sinks, norm_g):
    s = p_a.shape[0]
    w = WINDOW
    rows = SWA_STEP_BLOCKS * w
    kcol = ATTN_WIDTH // ATTN_KV_WIDTH
    before = lambda i: jnp.maximum(i * SWA_STEP_BLOCKS - 1, 0)
    return pl.pallas_call(
        _swa_kernel,
        grid=(s // rows,),
        in_specs=[
            pl.BlockSpec(memory_space=pltpu.SMEM),
            pl.BlockSpec((rows, ATTN_WIDTH), lambda i: (i, 0)),
            pl.BlockSpec((rows, ATTN_KV_WIDTH), lambda i: (i, kcol)),
            pl.BlockSpec((rows, ATTN_KV_WIDTH), lambda i: (i, kcol + 1)),
            pl.BlockSpec((w, ATTN_KV_WIDTH), lambda i: (before(i), kcol)),
            pl.BlockSpec((w, ATTN_KV_WIDTH), lambda i: (before(i), kcol + 1)),
            pl.BlockSpec((1, ATTN_WIDTH), lambda i: (0, 0)),
        ],
        out_specs=pl.BlockSpec((rows, ATTN_WIDTH), lambda i: (i, 0)),
        out_shape=jax.ShapeDtypeStruct((s, ATTN_WIDTH), BF16),
        compiler_params=pltpu.CompilerParams(dimension_semantics=("parallel",)),
        name="swa_sink_mix",
    )(sinks, p_a, p_a, p_a, p_a, p_a, norm_g)


def _gmlp_kernel(p_ref, lng_ref, lnb_ref, ws_ref, bias_ref, g_ref, o_ref):
    ch, hd = GM_CHUNK, GM_HEAD_DIM
    p = p_ref[...]
    z = 0.5 * p * (1.0 + lax.erf(p * (2.0 ** -0.5)))
    u = z[:, :GM_WIDTH]
    z2 = z[:, GM_WIDTH:]
    mu = jnp.mean(z2, axis=-1, keepdims=True)
    zc = z2 - mu
    var = jnp.mean(zc * zc, axis=-1, keepdims=True)
    z2 = (zc * lax.rsqrt(var + NORM_EPS) * lng_ref[...] + lnb_ref[...]).astype(BF16)
    ti = lax.broadcasted_iota(jnp.int32, (ch, ch), 0)
    si = lax.broadcasted_iota(jnp.int32, (ch, ch), 1)
    causal = si <= ti
    wh = [jnp.where(causal, ws_ref[h], 0.0).astype(BF16) for h in range(GM_HEADS)]
    bias = bias_ref[...]
    for ci in range(GM_STEP_CHUNKS):
        rows = slice(ci * ch, (ci + 1) * ch)
        mixed = [_dot(wh[h], z2[rows, h * hd:(h + 1) * hd]) for h in range(GM_HEADS)]
        y = u[rows, :] * (jnp.concatenate(mixed, axis=-1) + bias)
        o_ref[rows, :] = _rmsnorm(y, g_ref[...]).astype(o_ref.dtype)


def _gmlp_mix(p_g, ln_g, ln_b, ws, bias_full, norm_g):
    s = p_g.shape[0]
    ch = GM_CHUNK
    rows = GM_STEP_CHUNKS * ch
    row = pl.BlockSpec((1, GM_WIDTH), lambda i: (0, 0))
    return pl.pallas_call(
        _gmlp_kernel,
        grid=(s // rows,),
        in_specs=[
            pl.BlockSpec((rows, GM_COLS), lambda i: (i, 0)),
            row, row,
            pl.BlockSpec((GM_HEADS, ch, ch), lambda i: (0, 0, 0)),
            pl.BlockSpec((ch, GM_WIDTH), lambda i: (0, 0)),
            row,
        ],
        out_specs=pl.BlockSpec((rows, GM_WIDTH), lambda i: (i, 0)),
        out_shape=jax.ShapeDtypeStruct((s, GM_WIDTH), BF16),
        compiler_params=pltpu.CompilerParams(dimension_semantics=("parallel",)),
        name="chunk_gmlp_mix",
    )(p_g, ln_g, ln_b, ws, bias_full, norm_g)


def _out_proj_kernel(yr_ref, ya_ref, yg_ref, wr_ref, wa_ref, wg_ref, x_ref, o_ref):
    acc = _dot(yr_ref[...], wr_ref[...]) + _dot(ya_ref[...], wa_ref[...]) + _dot(yg_ref[...], wg_ref[...])
    o_ref[...] = x_ref[...] + acc


def _out_proj(y_r, y_a, y_g, w, layer, x, *, tm, tn):
    m, d = x.shape
    lhs = lambda n: pl.BlockSpec((tm, n), lambda i, j: (i, 0))
    rhs = lambda n, blk: pl.BlockSpec((None, n, tn), lambda i, j: (layer, blk, j))
    return pl.pallas_call(
        _out_proj_kernel,
        grid=(m // tm, d // tn),
        in_specs=[lhs(RWKV_WIDTH), lhs(ATTN_WIDTH), lhs(GM_WIDTH),
                  rhs(RWKV_WIDTH, 0), rhs(ATTN_WIDTH, 1), rhs(GM_WIDTH, (RWKV_WIDTH + ATTN_WIDTH) // GM_WIDTH),
                  pl.BlockSpec((tm, tn), lambda i, j: (i, j))],
        out_specs=pl.BlockSpec((tm, tn), lambda i, j: (i, j)),
        out_shape=jax.ShapeDtypeStruct((m, d), F32),
        compiler_params=pltpu.CompilerParams(
            dimension_semantics=("parallel", "parallel"), vmem_limit_bytes=V7X_VMEM_LIMIT),
        name="out_proj_residual",
    )(y_r, y_a, y_g, w, w, w, x)


def _ffn_kernel(x_ref, g_ref, up_ref, down_ref, gf_ref, o_ref, h_ref, *, final_norm):
    @pl.when(pl.program_id(1) == 0)
    def _():
        x = x_ref[...]
        h_ref[...] = _rmsnorm(x, g_ref[...]).astype(BF16)
        o_ref[...] = x

    a = jnp.maximum(_dot(h_ref[...], up_ref[...].astype(BF16)), 0.0)
    o_ref[...] += _dot((a * a).astype(BF16), down_ref[...].astype(BF16))

    if final_norm:
        @pl.when(pl.program_id(1) == pl.num_programs(1) - 1)
        def _():
            o_ref[...] = _rmsnorm(o_ref[...], gf_ref[...])


def _ffn(x, g, w_up, w_down, layer, g_final, *, final_norm, tm, tf):
    m, d = x.shape
    f = w_up.shape[2]
    return pl.pallas_call(
        functools.partial(_ffn_kernel, final_norm=final_norm),
        grid=(m // tm, f // tf),
        in_specs=[
            pl.BlockSpec((tm, d), lambda i, j: (i, 0)),
            pl.BlockSpec((1, d), lambda i, j: (0, 0)),
            pl.BlockSpec((None, d, tf), lambda i, j: (layer, 0, j)),
            pl.BlockSpec((None, tf, d), lambda i, j: (layer, j, 0)),
            pl.BlockSpec((1, d), lambda i, j: (0, 0)),
        ],
        out_specs=pl.BlockSpec((tm, d), lambda i, j: (i, 0)),
        out_shape=jax.ShapeDtypeStruct((m, d), F32),
        scratch_shapes=[pltpu.VMEM((tm, d), BF16)],
        compiler_params=pltpu.CompilerParams(
            dimension_semantics=("parallel", "arbitrary"), vmem_limit_bytes=V7X_VMEM_LIMIT),
        name="relu2_mlp_residual",
    )(x, g, w_up, w_down, g_final)


def kernel(x, ln1_g, w_in, rwkv_mu, rwkv_w0, rwkv_decay_up, rwkv_a0, rwkv_a_up, rwkv_g_up, rwkv_k_k,
           rwkv_k_a, rwkv_r_k, rwkv_lnx_g, rwkv_lnx_b, attn_sinks, attn_norm_g, gm_ln_g, gm_ln_b, gm_ws,
           gm_bs, gm_norm_g, w_out, ln2_g, w_ffn_up, w_ffn_down, lnf_g):
    batch, seq, d = x.shape
    assert (batch, seq, d) == (1, SEQ, D_MODEL)
    xs = x.reshape(seq, d)
    row = lambda a: a.reshape(1, -1)
    zeros_lora = jnp.zeros((LORA_IN - DECAY_LORA, RWKV_WIDTH), F32)
    w_in_b = w_in.astype(BF16)
    w_out_b = w_out.astype(BF16)
    for l in range(DEPTH):
        p_r, p_a, p_g = _in_proj(xs, row(ln1_g[l]), w_in_b, l, tm=512)

        wdec = jnp.concatenate([rwkv_decay_up[l], zeros_lora], axis=0).astype(BF16)
        wa = jnp.concatenate([zeros_lora, rwkv_a_up[l]], axis=0).astype(BF16)
        y_r = _rwkv_mix(p_r, row(rwkv_mu[l]), row(rwkv_w0[l]), wdec, row(rwkv_a0[l]), wa,
                        rwkv_g_up[l].astype(BF16), row(rwkv_k_k[l]), row(rwkv_k_a[l]), row(rwkv_r_k[l]),
                        row(rwkv_lnx_g[l]), row(rwkv_lnx_b[l]))
        y_a = _swa_mix(p_a, attn_sinks[l], row(attn_norm_g[l]))
        bias_full = jnp.repeat(gm_bs[l].T, GM_HEAD_DIM, axis=1)
        y_g = _gmlp_mix(p_g, row(gm_ln_g[l]), row(gm_ln_b[l]), gm_ws[l], bias_full, row(gm_norm_g[l]))

        xs = _out_proj(y_r, y_a, y_g, w_out_b, l, xs, tm=1024, tn=1024)
        xs = _ffn(xs, row(ln2_g[l]), w_ffn_up, w_ffn_down, l, row(lnf_g), final_norm=(l == DEPTH - 1),
                  tm=1024, tf=512)
    return xs.reshape(batch, seq, d)
```

```python
import functools

import jax
import jax.numpy as jnp
from jax import lax
from jax.experimental import pallas as pl
from jax.experimental.pallas import tpu as pltpu

F32 = jnp.float32
BF16 = jnp.bfloat16

D_MODEL = 2048
SEQ = 8192
DEPTH = 4
HEAD_DIM = 64
RWKV_HEADS = 12
RWKV_WIDTH = RWKV_HEADS * HEAD_DIM
DECAY_LORA = 64
ICLR_LORA = 64
GATE_LORA = 128
LORA_IN = DECAY_LORA + ICLR_LORA
ATTN_Q_HEADS = 12
ATTN_KV_HEADS = 4
ATTN_GROUP = ATTN_Q_HEADS // ATTN_KV_HEADS
ATTN_WIDTH = ATTN_Q_HEADS * HEAD_DIM
ATTN_KV_WIDTH = ATTN_KV_HEADS * HEAD_DIM
WINDOW = 128
SWA_STEP_BLOCKS = 2
GM_HEADS = 4
GM_HEAD_DIM = 128
GM_WIDTH = GM_HEADS * GM_HEAD_DIM
GM_CHUNK = 128
IN_PROJ_CHUNK = 256
MIX_WIDTH = RWKV_WIDTH + ATTN_WIDTH + GM_WIDTH
D_FF = 4 * D_MODEL
RWKV_COLS = 3 * RWKV_WIDTH + DECAY_LORA + ICLR_LORA + GATE_LORA
ATTN_COLS = ATTN_WIDTH + 2 * ATTN_KV_WIDTH
GM_COLS = 2 * GM_WIDTH
NORM_EPS = 1e-6
GN_EPS = 64e-5

RWKV_CHUNK = 64
RWKV_GROUP_HEADS = 4
RWKV_GROUP_WIDTH = RWKV_GROUP_HEADS * HEAD_DIM
RWKV_GROUPS = RWKV_HEADS // RWKV_GROUP_HEADS
RWKV_PAIR_CHUNKS = 2
RWKV_PAIR_ROWS = RWKV_PAIR_CHUNKS * RWKV_CHUNK
RWKV_STEP_PAIRS = 4
RWKV_STEP_ROWS = RWKV_STEP_PAIRS * RWKV_PAIR_ROWS
RWKV_CHUNK_LOG2 = 6
RWKV_INV_BASE_LOG2 = 3

V7X_VMEM_LIMIT = 56 * 1024 * 1024

_NN = (((1,), (0,)), ((), ()))
_NT = (((1,), (1,)), ((), ()))
_TN = (((0,), (0,)), ((), ()))


def _dot(a, b, dims=_NN):
    return lax.dot_general(a, b, dims, preferred_element_type=F32)


def _split_hi_lo(x):
    hi = x.astype(BF16)
    lo = (x - hi.astype(F32)).astype(BF16)
    return hi, lo


def _mm(a, b, dims=_NN):
    return _dot(a.astype(BF16), b.astype(BF16), dims)


def _rmsnorm(x, g):
    return x * lax.rsqrt(jnp.mean(x * x, axis=-1, keepdims=True) + NORM_EPS) * g


def _alternate(*iterators):
    live = list(iterators)
    while live:
        for it in list(live):
            try:
                next(it)
            except StopIteration:
                live.remove(it)


def _gmlp_rows(p, ln_g, ln_b, ws_ref, bias, norm_g):
    ch, hd = GM_CHUNK, GM_HEAD_DIM
    z = 0.5 * p * (1.0 + lax.erf(p * (2.0 ** -0.5)))
    u = z[:, :GM_WIDTH]
    z2 = z[:, GM_WIDTH:]
    mu = jnp.mean(z2, axis=-1, keepdims=True)
    zc = z2 - mu
    var = jnp.mean(zc * zc, axis=-1, keepdims=True)
    z2 = (zc * lax.rsqrt(var + NORM_EPS) * ln_g + ln_b).astype(BF16)
    ti = lax.broadcasted_iota(jnp.int32, (ch, ch), 0)
    si = lax.broadcasted_iota(jnp.int32, (ch, ch), 1)
    causal = si <= ti
    wh = [jnp.where(causal, ws_ref[h], 0.0).astype(BF16) for h in range(GM_HEADS)]
    out = []
    for ci in range(p.shape[0] // ch):
        rows = slice(ci * ch, (ci + 1) * ch)
        mixed = [_dot(wh[h], z2[rows, h * hd:(h + 1) * hd]) for h in range(GM_HEADS)]
        out.append(_rmsnorm(u[rows, :] * (jnp.concatenate(mixed, axis=-1) + bias), norm_g))
    return jnp.concatenate(out, axis=0)


def _in_proj_kernel(sink_ref, x_ref, xb_ref, g_ref, w_ref, lng_ref, lnb_ref, ws_ref, bias_ref, gmg_ref, ang_ref,
                    pr_ref, ya_ref, yg_ref):
    kv0 = RWKV_COLS + ATTN_WIDTH
    h = _rmsnorm(x_ref[...], g_ref[...]).astype(BF16)
    hb = _rmsnorm(xb_ref[...], g_ref[...]).astype(BF16)
    kv_before = _dot(hb, w_ref[:, kv0:kv0 + 2 * ATTN_KV_WIDTH]).astype(BF16)
    p_a = _dot(h, w_ref[:, RWKV_COLS:RWKV_COLS + ATTN_COLS]).astype(BF16)
    p_g = _dot(h, w_ref[:, RWKV_COLS + ATTN_COLS:])
    yg_ref[...] = _gmlp_rows(p_g, lng_ref[...], lnb_ref[...], ws_ref, bias_ref[...],
                             gmg_ref[...]).astype(yg_ref.dtype)

    def rwkv_columns():
        for c0 in range(0, RWKV_COLS, IN_PROJ_CHUNK):
            pr_ref[:, c0:c0 + IN_PROJ_CHUNK] = _dot(h, w_ref[:, c0:c0 + IN_PROJ_CHUNK])
            yield

    _alternate(_swa_stages(p_a, kv_before, pl.program_id(0) > 0, sink_ref, ang_ref[...], ya_ref),
               rwkv_columns())


def _in_proj_mix(x, g, w, layer, sinks, attn_norm_g, ln_g, ln_b, ws, bias_full, gm_norm_g, *, tm):
    m, d = x.shape
    n = w.shape[2]
    out = lambda width: pl.BlockSpec((tm, width), lambda i: (i, 0))
    row = lambda width: pl.BlockSpec((1, width), lambda i: (0, 0))
    return pl.pallas_call(
        _in_proj_kernel,
        grid=(m // tm,),
        in_specs=[
            pl.BlockSpec(memory_space=pltpu.SMEM),
            pl.BlockSpec((tm, d), lambda i: (i, 0)),
            pl.BlockSpec((WINDOW, d), lambda i: (jnp.maximum(i * (tm // WINDOW) - 1, 0), 0)),
            row(d),
            pl.BlockSpec((None, d, n), lambda i: (layer, 0, 0), pipeline_mode=pl.Buffered(1)),
            row(GM_WIDTH), row(GM_WIDTH),
            pl.BlockSpec((GM_HEADS, GM_CHUNK, GM_CHUNK), lambda i: (0, 0, 0)),
            pl.BlockSpec((GM_CHUNK, GM_WIDTH), lambda i: (0, 0)),
            row(GM_WIDTH), row(ATTN_WIDTH),
        ],
        out_specs=[out(RWKV_COLS), out(ATTN_WIDTH), out(GM_WIDTH)],
        out_shape=[jax.ShapeDtypeStruct((m, RWKV_COLS), F32), jax.ShapeDtypeStruct((m, ATTN_WIDTH), BF16),
                   jax.ShapeDtypeStruct((m, GM_WIDTH), BF16)],
        compiler_params=pltpu.CompilerParams(
            dimension_semantics=("parallel",), vmem_limit_bytes=V7X_VMEM_LIMIT),
        name="norm_in_proj_mix",
    )(sinks, x, x, g, w, ln_g, ln_b, ws, bias_full, gm_norm_g, attn_norm_g)


def _rwkv_kernel(p_ref, mu_ref, w0_ref, wdec_ref, a0_ref, wa_ref, wg_ref, kk_ref, ka_ref, rk_ref,
                 lng_ref, lnb_ref, o_ref, prev_ref, s_ref):
    c, gw, prows = RWKV_CHUNK, RWKV_GROUP_WIDTH, RWKV_PAIR_ROWS
    w = RWKV_WIDTH
    groups = range(RWKV_GROUPS)
    chunks = range(RWKV_PAIR_CHUNKS)
    units = [(ci, g) for ci in chunks for g in groups]

    @pl.when(pl.program_id(0) == 0)
    def _():
        prev_ref[...] = jnp.zeros_like(prev_ref)
        s_ref[...] = jnp.zeros_like(s_ref)

    rowh = lax.broadcasted_iota(jnp.int32, (gw, gw), 0) >> 6
    colh = lax.broadcasted_iota(jnp.int32, (gw, gw), 1) >> 6
    bdmask = rowh == colh
    head_ones = bdmask.astype(BF16)
    ti = lax.broadcasted_iota(jnp.int32, (prows, prows), 0)
    si = lax.broadcasted_iota(jnp.int32, (prows, prows), 1)
    tril_ones = ((si <= ti) & ((si >> 6) == (ti >> 6))).astype(BF16)
    t2 = lax.broadcasted_iota(jnp.int32, (c, gw), 0)
    s2 = lax.broadcasted_iota(jnp.int32, (c, gw), 1) & (c - 1)
    strict = s2 < t2
    incl = s2 <= t2
    eye4 = (s2 == t2).astype(F32)
    lane = lax.broadcasted_iota(jnp.int32, (c, 2 * HEAD_DIM), 1)
    even_head = (lane < HEAD_DIM).astype(BF16)
    odd_head = (lane >= HEAD_DIM).astype(BF16)
    zero_slab = jnp.zeros((c, 2 * HEAD_DIM), BF16)
    first_row = lax.broadcasted_iota(jnp.int32, (8, RWKV_COLS), 0) == 0

    def head_sum(x):
        return _dot(x.astype(BF16), head_ones)

    def bd(x):
        x = x.astype(BF16)
        lo, hi = x[:, :2 * HEAD_DIM], x[:, 2 * HEAD_DIM:]
        return jnp.concatenate([
            jnp.concatenate([lo * even_head, zero_slab], axis=1),
            jnp.concatenate([lo * odd_head, zero_slab], axis=1),
            jnp.concatenate([zero_slab, hi * even_head], axis=1),
            jnp.concatenate([zero_slab, hi * odd_head], axis=1)], axis=0)

    def cols(x, g):
        return x[:, g * gw:(g + 1) * gw]

    def chunk(x, ci):
        return x[ci * c:(ci + 1) * c, :]

    state = [s_ref[g] for g in groups]
    ops = {}

    def operands(pi):
        r0 = pi * prows
        p = p_ref[r0:r0 + prows, :]
        before = prev_ref[...] if pi == 0 else p_ref[r0 - 1:r0, :]
        rolled = pltpu.roll(p, 1, axis=0)
        shifted = jnp.concatenate([jnp.where(first_row, before, rolled[:8]), rolled[8:]], axis=0)
        p = p + (shifted - p) * mu_ref[...]
        yield
        x_lora = p[:, 3 * w:3 * w + LORA_IN]
        x_gate = p[:, 3 * w + LORA_IN:]
        dec = -(jax.nn.softplus(-(w0_ref[...] + _mm(jnp.tanh(x_lora), wdec_ref[...])))) - 0.5
        logd_all = -jnp.exp(dec)
        alpha_all = jax.nn.sigmoid(a0_ref[...] + _mm(x_lora, wa_ref[...]))
        o = ops[pi] = {"gate": _mm(jax.nn.sigmoid(x_gate), wg_ref[...])}
        for name in ("at", "rt", "bt", "kt", "v", "wend", "dend", "bonus"):
            o[name] = {}
        yield
        for g in groups:
            r, k, v = cols(p[:, 0:w], g), cols(p[:, w:2 * w], g), cols(p[:, 2 * w:3 * w], g)
            alpha, logd = cols(alpha_all, g), cols(logd_all, g)
            kk = k * cols(kk_ref[...], g)
            kk_sq = head_sum(kk * kk)
            dh, dl = _split_hi_lo(logd)
            cum = _dot(tril_ones, dh) + _dot(tril_ones, dl)
            k2 = k * (1.0 + (alpha - 1.0) * cols(ka_ref[...], g))
            o["bonus"][g] = head_sum(r * k2 * cols(rk_ref[...], g)) * v
            o["v"][g] = v.astype(BF16)
            yield
            kkn = kk / jnp.maximum(jnp.sqrt(kk_sq), 1e-12)
            b = kkn * alpha
            inv = jnp.exp(-cum)
            o["at"][g] = (-kkn * jnp.exp(cum - logd)).astype(BF16)
            o["rt"][g] = (r * jnp.exp(cum)).astype(BF16)
            yield
            o["bt"][g] = (b * inv).astype(BF16)
            o["kt"][g] = (k2 * inv).astype(BF16)
            yield
            for ci in chunks:
                cum_end = chunk(cum, ci)[c - 1:c, :]
                to_end = jnp.exp(cum_end - chunk(cum, ci))
                o["wend"][ci, g] = jnp.concatenate(
                    [(chunk(b, ci) * to_end).astype(BF16), (chunk(k2, ci) * to_end).astype(BF16)], axis=0)
                o["dend"][ci, g] = jnp.exp(cum_end)
            yield

    def head(pi):
        o = ops[pi]
        x, ab, ak, vq = {}, {}, {}, {}
        for (ci, g) in units:
            x[ci, g] = jnp.concatenate([chunk(o["at"][g], ci), chunk(o["rt"][g], ci)], axis=0)
            vq[ci, g] = chunk(o["v"][g], ci)
            ab[ci, g] = _dot(x[ci, g], bd(chunk(o["bt"][g], ci)), _NT)
            ak[ci, g] = _dot(x[ci, g], bd(chunk(o["kt"][g], ci)), _NT)
        yield
        a_ab = {u_: jnp.where(strict, ab[u_][:c], 0.0) for u_ in units}
        a_rb = {u_: jnp.where(incl, ab[u_][c:], 0.0) for u_ in units}
        a_ak = {u_: jnp.where(strict, ak[u_][:c], 0.0) for u_ in units}
        a_rk = {u_: jnp.where(incl, ak[u_][c:], 0.0) for u_ in units}

        blk_t, blk_s = t2 >> 3, s2 >> 3
        a_dg = {u_: jnp.where(blk_t == blk_s, a_ab[u_], 0.0) for u_ in units}
        t_inv = {u_: eye4 + a_dg[u_] for u_ in units}
        pw = {u_: _mm(a_dg[u_], bd(a_dg[u_])) for u_ in units}
        av = {u_: _mm(jnp.concatenate([a_ak[u_], a_rk[u_]], axis=0), bd(vq[u_])) for u_ in units}
        yield
        tp = {u_: _mm(jnp.concatenate([t_inv[u_], pw[u_]], axis=0), bd(pw[u_])) for u_ in units}
        yield
        t_inv = {u_: t_inv[u_] + tp[u_][:c] for u_ in units}
        t_inv = {u_: t_inv[u_] + _mm(t_inv[u_], bd(tp[u_][c:])) for u_ in units}
        yield
        for shift in range(RWKV_INV_BASE_LOG2, RWKV_CHUNK_LOG2):
            couple = ((t2 >> (shift + 1)) == (s2 >> (shift + 1))) & ((t2 >> shift) != (s2 >> shift))
            de = {u_: _mm(t_inv[u_], bd(jnp.where(couple, a_ab[u_], 0.0))) for u_ in units}
            yield
            t_inv = {u_: t_inv[u_] + _mm(de[u_], bd(t_inv[u_])) for u_ in units}
            yield
        o.update(x=x, vq=vq, a_rb=a_rb, av=av, t_inv=t_inv)

    def tail(pi):
        o = ops[pi]
        x, vq, a_rb, av, t_inv = o["x"], o["vq"], o["a_rb"], o["av"], o["t_inv"]
        y = {}
        for ci in chunks:
            xs = [_mm(x[ci, g], state[g], _NT) for g in groups]
            yield
            u = [_mm(t_inv[ci, g], bd(xs[g][:c] + av[ci, g][:c])) for g in groups]
            yield
            for g in groups:
                y[ci, g] = xs[g][c:] + _mm(a_rb[ci, g], bd(u[g])) + av[ci, g][c:]
            for g in groups:
                upd = _dot(jnp.concatenate([u[g].astype(BF16), vq[ci, g]], axis=0), o["wend"][ci, g], _TN)
                state[g] = state[g] * o["dend"][ci, g] + jnp.where(bdmask, upd, 0.0)
            yield

        yf = [jnp.concatenate([y[ci, g] for ci in chunks], axis=0) for g in groups]
        mean = [head_sum(yf[g]) * (1.0 / HEAD_DIM) for g in groups]
        yield
        yc = [yf[g] - mean[g] for g in groups]
        var = [head_sum(yc[g] * yc[g]) * (1.0 / HEAD_DIM) for g in groups]
        yield
        r0 = pi * prows
        for g in groups:
            yn = yc[g] * lax.rsqrt(var[g] + GN_EPS) * cols(lng_ref[...], g) + cols(lnb_ref[...], g)
            o_ref[r0:r0 + prows, g * gw:(g + 1) * gw] = (
                (yn + o["bonus"][g]) * cols(o["gate"], g)).astype(o_ref.dtype)
        del ops[pi]

    npairs = RWKV_STEP_PAIRS
    _alternate(operands(0))
    _alternate(head(0), *([operands(1)] if npairs > 1 else []))
    for pi in range(npairs):
        _alternate(tail(pi), *([head(pi + 1)] if pi + 1 < npairs else []),
                   *([operands(pi + 2)] if pi + 2 < npairs else []))
    prev_ref[...] = p_ref[RWKV_STEP_ROWS - 1:RWKV_STEP_ROWS, :]
    for g in groups:
        s_ref[g] = state[g]


def _rwkv_mix(p_r, mu, w0, wdec, a0, wa, wg, k_k, k_a, r_k, lnx_g, lnx_b):
    s = p_r.shape[0]
    rows = RWKV_STEP_ROWS
    row = lambda n: pl.BlockSpec((1, n), lambda i: (0, 0))
    full = lambda a: pl.BlockSpec(a.shape, lambda i: (0, 0))
    return pl.pallas_call(
        _rwkv_kernel,
        grid=(s // rows,),
        in_specs=[
            pl.BlockSpec((rows, RWKV_COLS), lambda i: (i, 0)),
            row(RWKV_COLS), row(RWKV_WIDTH), full(wdec), row(RWKV_WIDTH), full(wa), full(wg),
            row(RWKV_WIDTH), row(RWKV_WIDTH), row(RWKV_WIDTH), row(RWKV_WIDTH), row(RWKV_WIDTH),
        ],
        out_specs=pl.BlockSpec((rows, RWKV_WIDTH), lambda i: (i, 0)),
        out_shape=jax.ShapeDtypeStruct((s, RWKV_WIDTH), BF16),
        scratch_shapes=[
            pltpu.VMEM((1, RWKV_COLS), F32),
            pltpu.VMEM((RWKV_GROUPS, RWKV_GROUP_WIDTH, RWKV_GROUP_WIDTH), F32),
        ],
        compiler_params=pltpu.CompilerParams(
            dimension_semantics=("arbitrary",), vmem_limit_bytes=V7X_VMEM_LIMIT),
        name="rwkv7_mix",
    )(p_r, mu, w0, wdec, a0, wa, wg, k_k, k_a, r_k, lnx_g, lnx_b)


def _swa_stages(p_a, kv_before, has_before, sink_ref, norm_g, o_ref):
    w, hd = WINDOW, HEAD_DIM
    heads = [[kv * ATTN_GROUP + j for j in range(ATTN_GROUP)] for kv in range(ATTN_KV_HEADS)]
    qi = lax.broadcasted_iota(jnp.int32, (w, 2 * w), 0)
    kj = lax.broadcasted_iota(jnp.int32, (w, 2 * w), 1)
    band = (kj > qi) & (kj <= qi + w)
    first = band & ((kj >= w) | has_before)
    stack = lambda m: jnp.concatenate([jnp.where(m, 0.0, -jnp.inf)] * ATTN_GROUP, axis=0)
    neg_first, neg_band = stack(first), stack(band)
    grp_row = lax.broadcasted_iota(jnp.int32, (ATTN_GROUP * w, 1), 0) >> 7
    q = p_a[:, :ATTN_WIDTH] * (hd ** -0.5)
    ones = jnp.ones((2 * w, hd), BF16)

    def keys(col0, b, kv):
        cur = slice(col0 + kv * hd, col0 + (kv + 1) * hd)
        old = slice(col0 - ATTN_WIDTH + kv * hd, col0 - ATTN_WIDTH + (kv + 1) * hd)
        prev = kv_before[:, old] if b == 0 else p_a[(b - 1) * w:b * w, cur]
        return jnp.concatenate([prev, p_a[b * w:(b + 1) * w, cur]], axis=0)

    sink = []
    for kv in range(ATTN_KV_HEADS):
        sk = jnp.full((ATTN_GROUP * w, 1), sink_ref[heads[kv][0]], F32)
        for j in range(1, ATTN_GROUP):
            sk = jnp.where(grp_row == j, sink_ref[heads[kv][j]], sk)
        sink.append(sk)
    for b0 in range(0, p_a.shape[0] // w, SWA_STEP_BLOCKS):
        units = [(b, kv) for b in range(b0, b0 + SWA_STEP_BLOCKS) for kv in range(ATTN_KV_HEADS)]
        s = {}
        for (b, kv) in units:
            qs = jnp.concatenate([q[b * w:(b + 1) * w, h * hd:(h + 1) * hd] for h in heads[kv]], axis=0)
            s[b, kv] = _dot(qs, keys(ATTN_WIDTH, b, kv), _NT)
        yield
        e, sink_e = {}, {}
        for (b, kv) in units:
            sm = s[b, kv] + (neg_first if b == 0 else neg_band)
            m = jnp.maximum(jnp.max(sm, axis=-1, keepdims=True), sink[kv])
            e[b, kv] = jnp.exp(sm - m).astype(BF16)
            sink_e[b, kv] = jnp.exp(sink[kv] - m)
        yield
        o = {}
        for (b, kv) in units:
            pv = _dot(e[b, kv], jnp.concatenate([keys(ATTN_WIDTH + ATTN_KV_WIDTH, b, kv), ones], axis=1))
            denom = pltpu.roll(pv, hd, axis=1) + sink_e[b, kv]
            o[b, kv] = (pv / denom)[:, :hd]
        yield
        for b in range(b0, b0 + SWA_STEP_BLOCKS):
            outs = [None] * ATTN_Q_HEADS
            for kv in range(ATTN_KV_HEADS):
                for j, h in enumerate(heads[kv]):
                    outs[h] = o[b, kv][j * w:(j + 1) * w, :]
            y = jnp.concatenate(outs, axis=-1)
            o_ref[b * w:(b + 1) * w, :] = _rmsnorm(y, norm_g).astype(o_ref.dtype)
        yield


def _out_proj_kernel(yr_ref, ya_ref, yg_ref, wr_ref, wa_ref, wg_ref, x_ref, o_ref):
    acc = _dot(yr_ref[...], wr_ref[...]) + _dot(ya_ref[...], wa_ref[...]) + _dot(yg_ref[...], wg_ref[...])
    o_ref[...] = x_ref[...] + acc


def _out_proj(y_r, y_a, y_g, w, layer, x, *, tm):
    m, d = x.shape
    lhs = lambda n: pl.BlockSpec((tm, n), lambda i: (i, 0))
    rhs = lambda n, blk: pl.BlockSpec((None, n, d), lambda i: (layer, blk, 0), pipeline_mode=pl.Buffered(1))
    return pl.pallas_call(
        _out_proj_kernel,
        grid=(m // tm,),
        in_specs=[lhs(RWKV_WIDTH), lhs(ATTN_WIDTH), lhs(GM_WIDTH),
                  rhs(RWKV_WIDTH, 0), rhs(ATTN_WIDTH, 1), rhs(GM_WIDTH, (RWKV_WIDTH + ATTN_WIDTH) // GM_WIDTH),
                  pl.BlockSpec((tm, d), lambda i: (i, 0))],
        out_specs=pl.BlockSpec((tm, d), lambda i: (i, 0)),
        out_shape=jax.ShapeDtypeStruct((m, d), F32),
        compiler_params=pltpu.CompilerParams(
            dimension_semantics=("parallel",), vmem_limit_bytes=V7X_VMEM_LIMIT),
        name="out_proj_residual",
    )(y_r, y_a, y_g, w, w, w, x)


def _ffn_kernel(x_ref, g_ref, up_ref, down_ref, gf_ref, o_ref, h_ref, *, final_norm):
    @pl.when(pl.program_id(1) == 0)
    def _():
        x = x_ref[...]
        h_ref[...] = _rmsnorm(x, g_ref[...]).astype(BF16)
        o_ref[...] = x

    a = jnp.maximum(_dot(h_ref[...], up_ref[...].astype(BF16)), 0.0)
    o_ref[...] += _dot((a * a).astype(BF16), down_ref[...].astype(BF16))

    if final_norm:
        @pl.when(pl.program_id(1) == pl.num_programs(1) - 1)
        def _():
            o_ref[...] = _rmsnorm(o_ref[...], gf_ref[...])


def _ffn(x, g, w_up, w_down, layer, g_final, *, final_norm, tm, tf):
    m, d = x.shape
    f = w_up.shape[2]
    return pl.pallas_call(
        functools.partial(_ffn_kernel, final_norm=final_norm),
        grid=(m // tm, f // tf),
        in_specs=[
            pl.BlockSpec((tm, d), lambda i, j: (i, 0)),
            pl.BlockSpec((1, d), lambda i, j: (0, 0)),
            pl.BlockSpec((None, d, tf), lambda i, j: (layer, 0, j)),
            pl.BlockSpec((None, tf, d), lambda i, j: (layer, j, 0)),
            pl.BlockSpec((1, d), lambda i, j: (0, 0)),
        ],
        out_specs=pl.BlockSpec((tm, d), lambda i, j: (i, 0)),
        out_shape=jax.ShapeDtypeStruct((m, d), F32),
        scratch_shapes=[pltpu.VMEM((tm, d), BF16)],
        compiler_params=pltpu.CompilerParams(
            dimension_semantics=("parallel", "arbitrary"), vmem_limit_bytes=V7X_VMEM_LIMIT),
        name="relu2_mlp_residual",
    )(x, g, w_up, w_down, g_final)


def kernel(x, ln1_g, w_in, rwkv_mu, rwkv_w0, rwkv_decay_up, rwkv_a0, rwkv_a_up, rwkv_g_up, rwkv_k_k,
           rwkv_k_a, rwkv_r_k, rwkv_lnx_g, rwkv_lnx_b, attn_sinks, attn_norm_g, gm_ln_g, gm_ln_b, gm_ws,
           gm_bs, gm_norm_g, w_out, ln2_g, w_ffn_up, w_ffn_down, lnf_g):
    batch, seq, d = x.shape
    assert (batch, seq, d) == (1, SEQ, D_MODEL)
    xs = x.reshape(seq, d)
    row = lambda a: a.reshape(1, -1)
    zeros_lora = jnp.zeros((LORA_IN - DECAY_LORA, RWKV_WIDTH), F32)
    w_in_b = w_in.astype(BF16)
    w_out_b = w_out.astype(BF16)
    for l in range(DEPTH):
        bias_full = jnp.repeat(gm_bs[l].T, GM_HEAD_DIM, axis=1)
        p_r, y_a, y_g = _in_proj_mix(xs, row(ln1_g[l]), w_in_b, l, attn_sinks[l], row(attn_norm_g[l]),
                                     row(gm_ln_g[l]), row(gm_ln_b[l]), gm_ws[l], bias_full,
                                     row(gm_norm_g[l]), tm=512)

        wdec = jnp.concatenate([rwkv_decay_up[l], zeros_lora], axis=0).astype(BF16)
        wa = jnp.concatenate([zeros_lora, rwkv_a_up[l]], axis=0).astype(BF16)
        y_r = _rwkv_mix(p_r, row(rwkv_mu[l]), row(rwkv_w0[l]), wdec, row(rwkv_a0[l]), wa,
                        rwkv_g_up[l].astype(BF16), row(rwkv_k_k[l]), row(rwkv_k_a[l]), row(rwkv_r_k[l]),
                        row(rwkv_lnx_g[l]), row(rwkv_lnx_b[l]))

        xs = _out_proj(y_r, y_a, y_g, w_out_b, l, xs, tm=512)
        xs = _ffn(xs, row(ln2_g[l]), w_ffn_up, w_ffn_down, l, row(lnf_g), final_norm=(l == DEPTH - 1),
                  tm=1024, tf=512)
    return xs.reshape(batch, seq, d)
```

```python
import functools

import jax
import jax.numpy as jnp
from jax import lax
from jax.experimental import pallas as pl
from jax.experimental.pallas import tpu as pltpu

F32 = jnp.float32
BF16 = jnp.bfloat16

D_MODEL = 2048
SEQ = 8192
DEPTH = 4
HEAD_DIM = 64
RWKV_HEADS = 12
RWKV_WIDTH = RWKV_HEADS * HEAD_DIM
DECAY_LORA = 64
ICLR_LORA = 64
GATE_LORA = 128
LORA_IN = DECAY_LORA + ICLR_LORA
ATTN_Q_HEADS = 12
ATTN_KV_HEADS = 4
ATTN_GROUP = ATTN_Q_HEADS // ATTN_KV_HEADS
ATTN_WIDTH = ATTN_Q_HEADS * HEAD_DIM
ATTN_KV_WIDTH = ATTN_KV_HEADS * HEAD_DIM
WINDOW = 128
SWA_STEP_BLOCKS = 2
GM_HEADS = 4
GM_HEAD_DIM = 128
GM_WIDTH = GM_HEADS * GM_HEAD_DIM
GM_CHUNK = 128
IN_PROJ_CHUNK = 256
MIX_WIDTH = RWKV_WIDTH + ATTN_WIDTH + GM_WIDTH
D_FF = 4 * D_MODEL
RWKV_COLS = 3 * RWKV_WIDTH + DECAY_LORA + ICLR_LORA + GATE_LORA
ATTN_COLS = ATTN_WIDTH + 2 * ATTN_KV_WIDTH
GM_COLS = 2 * GM_WIDTH
NORM_EPS = 1e-6
GN_EPS = 64e-5

RWKV_CHUNK = 64
RWKV_GROUP_HEADS = 4
RWKV_GROUP_WIDTH = RWKV_GROUP_HEADS * HEAD_DIM
RWKV_GROUPS = RWKV_HEADS // RWKV_GROUP_HEADS
RWKV_PAIR_CHUNKS = 2
RWKV_PAIR_ROWS = RWKV_PAIR_CHUNKS * RWKV_CHUNK
RWKV_STEP_PAIRS = 4
RWKV_STEP_ROWS = RWKV_STEP_PAIRS * RWKV_PAIR_ROWS
RWKV_CHUNK_LOG2 = 6
RWKV_INV_BASE_LOG2 = 3

V7X_VMEM_LIMIT = 56 * 1024 * 1024

_NN = (((1,), (0,)), ((), ()))
_NT = (((1,), (1,)), ((), ()))
_TN = (((0,), (0,)), ((), ()))


def _dot(a, b, dims=_NN):
    return lax.dot_general(a, b, dims, preferred_element_type=F32)


def _split_hi_lo(x):
    hi = x.astype(BF16)
    lo = (x - hi.astype(F32)).astype(BF16)
    return hi, lo


def _mm(a, b, dims=_NN):
    return _dot(a.astype(BF16), b.astype(BF16), dims)


def _rmsnorm(x, g):
    return x * lax.rsqrt(jnp.mean(x * x, axis=-1, keepdims=True) + NORM_EPS) * g


def _alternate(*iterators):
    live = list(iterators)
    while live:
        for it in list(live):
            try:
                next(it)
            except StopIteration:
                live.remove(it)


def _gmlp_rows(p, ln_g, ln_b, ws_ref, bias, norm_g):
    ch, hd = GM_CHUNK, GM_HEAD_DIM
    z = 0.5 * p * (1.0 + lax.erf(p * (2.0 ** -0.5)))
    u = z[:, :GM_WIDTH]
    z2 = z[:, GM_WIDTH:]
    mu = jnp.mean(z2, axis=-1, keepdims=True)
    zc = z2 - mu
    var = jnp.mean(zc * zc, axis=-1, keepdims=True)
    z2 = (zc * lax.rsqrt(var + NORM_EPS) * ln_g + ln_b).astype(BF16)
    ti = lax.broadcasted_iota(jnp.int32, (ch, ch), 0)
    si = lax.broadcasted_iota(jnp.int32, (ch, ch), 1)
    causal = si <= ti
    wh = [jnp.where(causal, ws_ref[h], 0.0).astype(BF16) for h in range(GM_HEADS)]
    out = []
    for ci in range(p.shape[0] // ch):
        rows = slice(ci * ch, (ci + 1) * ch)
        mixed = [_dot(wh[h], z2[rows, h * hd:(h + 1) * hd]) for h in range(GM_HEADS)]
        out.append(_rmsnorm(u[rows, :] * (jnp.concatenate(mixed, axis=-1) + bias), norm_g))
    return jnp.concatenate(out, axis=0)


def _in_proj_kernel(sink_ref, x_ref, xb_ref, g_ref, w_ref, lng_ref, lnb_ref, ws_ref, bias_ref, gmg_ref, ang_ref,
                    pr_ref, ya_ref, yg_ref):
    kv0 = RWKV_COLS + ATTN_WIDTH
    h = _rmsnorm(x_ref[...], g_ref[...]).astype(BF16)
    hb = _rmsnorm(xb_ref[...], g_ref[...]).astype(BF16)
    kv_before = _dot(hb, w_ref[:, kv0:kv0 + 2 * ATTN_KV_WIDTH]).astype(BF16)
    p_a = _dot(h, w_ref[:, RWKV_COLS:RWKV_COLS + ATTN_COLS]).astype(BF16)
    p_g = _dot(h, w_ref[:, RWKV_COLS + ATTN_COLS:])
    yg_ref[...] = _gmlp_rows(p_g, lng_ref[...], lnb_ref[...], ws_ref, bias_ref[...],
                             gmg_ref[...]).astype(yg_ref.dtype)

    def rwkv_columns():
        for c0 in range(0, RWKV_COLS, IN_PROJ_CHUNK):
            pr_ref[:, c0:c0 + IN_PROJ_CHUNK] = _dot(h, w_ref[:, c0:c0 + IN_PROJ_CHUNK])
            yield

    _alternate(_swa_stages(p_a, kv_before, pl.program_id(0) > 0, sink_ref, ang_ref[...], ya_ref),
               rwkv_columns())


def _in_proj_mix(x, g, w, layer, sinks, attn_norm_g, ln_g, ln_b, ws, bias_full, gm_norm_g, *, tm):
    m, d = x.shape
    n = w.shape[2]
    out = lambda width: pl.BlockSpec((tm, width), lambda i: (i, 0))
    row = lambda width: pl.BlockSpec((1, width), lambda i: (0, 0))
    return pl.pallas_call(
        _in_proj_kernel,
        grid=(m // tm,),
        in_specs=[
            pl.BlockSpec(memory_space=pltpu.SMEM),
            pl.BlockSpec((tm, d), lambda i: (i, 0)),
            pl.BlockSpec((WINDOW, d), lambda i: (jnp.maximum(i * (tm // WINDOW) - 1, 0), 0)),
            row(d),
            pl.BlockSpec((None, d, n), lambda i: (layer, 0, 0), pipeline_mode=pl.Buffered(1)),
            row(GM_WIDTH), row(GM_WIDTH),
            pl.BlockSpec((GM_HEADS, GM_CHUNK, GM_CHUNK), lambda i: (0, 0, 0)),
            pl.BlockSpec((GM_CHUNK, GM_WIDTH), lambda i: (0, 0)),
            row(GM_WIDTH), row(ATTN_WIDTH),
        ],
        out_specs=[out(RWKV_COLS), out(ATTN_WIDTH), out(GM_WIDTH)],
        out_shape=[jax.ShapeDtypeStruct((m, RWKV_COLS), F32), jax.ShapeDtypeStruct((m, ATTN_WIDTH), BF16),
                   jax.ShapeDtypeStruct((m, GM_WIDTH), BF16)],
        compiler_params=pltpu.CompilerParams(
            dimension_semantics=("parallel",), vmem_limit_bytes=V7X_VMEM_LIMIT),
        name="norm_in_proj_mix",
    )(sinks, x, x, g, w, ln_g, ln_b, ws, bias_full, gm_norm_g, attn_norm_g)


def _rwkv_kernel(p_ref, mu_ref, w0_ref, wdec_ref, a0_ref, wa_ref, wg_ref, kk_ref, ka_ref, rk_ref,
                 lng_ref, lnb_ref, o_ref, prev_ref, s_ref):
    c, gw, prows = RWKV_CHUNK, RWKV_GROUP_WIDTH, RWKV_PAIR_ROWS
    w = RWKV_WIDTH
    groups = range(RWKV_GROUPS)
    chunks = range(RWKV_PAIR_CHUNKS)
    units = [(ci, g) for ci in chunks for g in groups]

    @pl.when(pl.program_id(0) == 0)
    def _():
        prev_ref[...] = jnp.zeros_like(prev_ref)
        s_ref[...] = jnp.zeros_like(s_ref)

    rowh = lax.broadcasted_iota(jnp.int32, (gw, gw), 0) >> 6
    colh = lax.broadcasted_iota(jnp.int32, (gw, gw), 1) >> 6
    bdmask = rowh == colh
    head_ones = bdmask.astype(BF16)
    ti = lax.broadcasted_iota(jnp.int32, (prows, prows), 0)
    si = lax.broadcasted_iota(jnp.int32, (prows, prows), 1)
    tril_ones = ((si <= ti) & ((si >> 6) == (ti >> 6))).astype(BF16)
    t2 = lax.broadcasted_iota(jnp.int32, (c, gw), 0)
    s2 = lax.broadcasted_iota(jnp.int32, (c, gw), 1) & (c - 1)
    strict = s2 < t2
    incl = s2 <= t2
    eye4 = (s2 == t2).astype(F32)
    lane = lax.broadcasted_iota(jnp.int32, (c, 2 * HEAD_DIM), 1)
    even_head = (lane < HEAD_DIM).astype(BF16)
    odd_head = (lane >= HEAD_DIM).astype(BF16)
    zero_slab = jnp.zeros((c, 2 * HEAD_DIM), BF16)
    first_row = lax.broadcasted_iota(jnp.int32, (8, RWKV_COLS), 0) == 0

    def head_sum(x):
        return _dot(x.astype(BF16), head_ones)

    def bd(x):
        x = x.astype(BF16)
        lo, hi = x[:, :2 * HEAD_DIM], x[:, 2 * HEAD_DIM:]
        return jnp.concatenate([
            jnp.concatenate([lo * even_head, zero_slab], axis=1),
            jnp.concatenate([lo * odd_head, zero_slab], axis=1),
            jnp.concatenate([zero_slab, hi * even_head], axis=1),
            jnp.concatenate([zero_slab, hi * odd_head], axis=1)], axis=0)

    def cols(x, g):
        return x[:, g * gw:(g + 1) * gw]

    def chunk(x, ci):
        return x[ci * c:(ci + 1) * c, :]

    state = [s_ref[g] for g in groups]
    ops = {}

    def operands(pi):
        r0 = pi * prows
        p = p_ref[r0:r0 + prows, :]
        before = prev_ref[...] if pi == 0 else p_ref[r0 - 1:r0, :]
        rolled = pltpu.roll(p, 1, axis=0)
        shifted = jnp.concatenate([jnp.where(first_row, before, rolled[:8]), rolled[8:]], axis=0)
        p = p + (shifted - p) * mu_ref[...]
        yield
        x_lora = p[:, 3 * w:3 * w + LORA_IN]
        x_gate = p[:, 3 * w + LORA_IN:]
        dec = -(jax.nn.softplus(-(w0_ref[...] + _mm(jnp.tanh(x_lora), wdec_ref[...])))) - 0.5
        logd_all = -jnp.exp(dec)
        alpha_all = jax.nn.sigmoid(a0_ref[...] + _mm(x_lora, wa_ref[...]))
        o = ops[pi] = {"gate": _mm(jax.nn.sigmoid(x_gate), wg_ref[...])}
        for name in ("at", "rt", "bt", "kt", "v", "wend", "dend", "bonus"):
            o[name] = {}
        yield
        for g in groups:
            r, k, v = cols(p[:, 0:w], g), cols(p[:, w:2 * w], g), cols(p[:, 2 * w:3 * w], g)
            alpha, logd = cols(alpha_all, g), cols(logd_all, g)
            kk = k * cols(kk_ref[...], g)
            kk_sq = head_sum(kk * kk)
            dh, dl = _split_hi_lo(logd)
            cum = _dot(tril_ones, dh) + _dot(tril_ones, dl)
            k2 = k * (1.0 + (alpha - 1.0) * cols(ka_ref[...], g))
            o["bonus"][g] = head_sum(r * k2 * cols(rk_ref[...], g)) * v
            o["v"][g] = v.astype(BF16)
            yield
            kkn = kk / jnp.maximum(jnp.sqrt(kk_sq), 1e-12)
            b = kkn * alpha
            inv = jnp.exp(-cum)
            o["at"][g] = (-kkn * jnp.exp(cum - logd)).astype(BF16)
            o["rt"][g] = (r * jnp.exp(cum)).astype(BF16)
            yield
            o["bt"][g] = (b * inv).astype(BF16)
            o["kt"][g] = (k2 * inv).astype(BF16)
            yield
            for ci in chunks:
                cum_end = chunk(cum, ci)[c - 1:c, :]
                to_end = jnp.exp(cum_end - chunk(cum, ci))
                o["wend"][ci, g] = jnp.concatenate(
                    [(chunk(b, ci) * to_end).astype(BF16), (chunk(k2, ci) * to_end).astype(BF16)], axis=0)
                o["dend"][ci, g] = jnp.exp(cum_end)
            yield

    def head(pi):
        o = ops[pi]
        x, ab, ak, vq = {}, {}, {}, {}
        for (ci, g) in units:
            x[ci, g] = jnp.concatenate([chunk(o["at"][g], ci), chunk(o["rt"][g], ci)], axis=0)
            vq[ci, g] = chunk(o["v"][g], ci)
            ab[ci, g] = _dot(x[ci, g], bd(chunk(o["bt"][g], ci)), _NT)
            ak[ci, g] = _dot(x[ci, g], bd(chunk(o["kt"][g], ci)), _NT)
        yield
        a_ab = {u_: jnp.where(strict, ab[u_][:c], 0.0) for u_ in units}
        a_rb = {u_: jnp.where(incl, ab[u_][c:], 0.0) for u_ in units}
        a_ak = {u_: jnp.where(strict, ak[u_][:c], 0.0) for u_ in units}
        a_rk = {u_: jnp.where(incl, ak[u_][c:], 0.0) for u_ in units}

        blk_t, blk_s = t2 >> 3, s2 >> 3
        a_dg = {u_: jnp.where(blk_t == blk_s, a_ab[u_], 0.0) for u_ in units}
        t_inv = {u_: eye4 + a_dg[u_] for u_ in units}
        pw = {u_: _mm(a_dg[u_], bd(a_dg[u_])) for u_ in units}
        av = {u_: _mm(jnp.concatenate([a_ak[u_], a_rk[u_]], axis=0), bd(vq[u_])) for u_ in units}
        yield
        tp = {u_: _mm(jnp.concatenate([t_inv[u_], pw[u_]], axis=0), bd(pw[u_])) for u_ in units}
        yield
        t_inv = {u_: t_inv[u_] + tp[u_][:c] for u_ in units}
        t_inv = {u_: t_inv[u_] + _mm(t_inv[u_], bd(tp[u_][c:])) for u_ in units}
        yield
        for shift in range(RWKV_INV_BASE_LOG2, RWKV_CHUNK_LOG2):
            couple = ((t2 >> (shift + 1)) == (s2 >> (shift + 1))) & ((t2 >> shift) != (s2 >> shift))
            de = {u_: _mm(t_inv[u_], bd(jnp.where(couple, a_ab[u_], 0.0))) for u_ in units}
            yield
            t_inv = {u_: t_inv[u_] + _mm(de[u_], bd(t_inv[u_])) for u_ in units}
            yield
        o.update(x=x, vq=vq, a_rb=a_rb, av=av, t_inv=t_inv)

    def tail(pi):
        o = ops[pi]
        x, vq, a_rb, av, t_inv = o["x"], o["vq"], o["a_rb"], o["av"], o["t_inv"]
        y = {}
        for ci in chunks:
            xs = [_mm(x[ci, g], state[g], _NT) for g in groups]
            yield
            u = [_mm(t_inv[ci, g], bd(xs[g][:c] + av[ci, g][:c])) for g in groups]
            yield
            for g in groups:
                y[ci, g] = xs[g][c:] + _mm(a_rb[ci, g], bd(u[g])) + av[ci, g][c:]
            for g in groups:
                upd = _dot(jnp.concatenate([u[g].astype(BF16), vq[ci, g]], axis=0), o["wend"][ci, g], _TN)
                state[g] = state[g] * o["dend"][ci, g] + jnp.where(bdmask, upd, 0.0)
            yield

        yf = [jnp.concatenate([y[ci, g] for ci in chunks], axis=0) for g in groups]
        mean = [head_sum(yf[g]) * (1.0 / HEAD_DIM) for g in groups]
        yield
        yc = [yf[g] - mean[g] for g in groups]
        var = [head_sum(yc[g] * yc[g]) * (1.0 / HEAD_DIM) for g in groups]
        yield
        r0 = pi * prows
        for g in groups:
            yn = yc[g] * lax.rsqrt(var[g] + GN_EPS) * cols(lng_ref[...], g) + cols(lnb_ref[...], g)
            o_ref[r0:r0 + prows, g * gw:(g + 1) * gw] = (
                (yn + o["bonus"][g]) * cols(o["gate"], g)).astype(o_ref.dtype)
        del ops[pi]

    npairs = RWKV_STEP_PAIRS
    _alternate(operands(0))
    _alternate(head(0), *([operands(1)] if npairs > 1 else []))
    for pi in range(npairs):
        _alternate(*([head(pi + 1)] if pi + 1 < npairs else []), tail(pi),
                   *([operands(pi + 2)] if pi + 2 < npairs else []))
    prev_ref[...] = p_ref[RWKV_STEP_ROWS - 1:RWKV_STEP_ROWS, :]
    for g in groups:
        s_ref[g] = state[g]


def _rwkv_mix(p_r, mu, w0, wdec, a0, wa, wg, k_k, k_a, r_k, lnx_g, lnx_b):
    s = p_r.shape[0]
    rows = RWKV_STEP_ROWS
    row = lambda n: pl.BlockSpec((1, n), lambda i: (0, 0))
    full = lambda a: pl.BlockSpec(a.shape, lambda i: (0, 0))
    return pl.pallas_call(
        _rwkv_kernel,
        grid=(s // rows,),
        in_specs=[
            pl.BlockSpec((rows, RWKV_COLS), lambda i: (i, 0)),
            row(RWKV_COLS), row(RWKV_WIDTH), full(wdec), row(RWKV_WIDTH), full(wa), full(wg),
            row(RWKV_WIDTH), row(RWKV_WIDTH), row(RWKV_WIDTH), row(RWKV_WIDTH), row(RWKV_WIDTH),
        ],
        out_specs=pl.BlockSpec((rows, RWKV_WIDTH), lambda i: (i, 0)),
        out_shape=jax.ShapeDtypeStruct((s, RWKV_WIDTH), BF16),
        scratch_shapes=[
            pltpu.VMEM((1, RWKV_COLS), F32),
            pltpu.VMEM((RWKV_GROUPS, RWKV_GROUP_WIDTH, RWKV_GROUP_WIDTH), F32),
        ],
        compiler_params=pltpu.CompilerParams(
            dimension_semantics=("arbitrary",), vmem_limit_bytes=V7X_VMEM_LIMIT),
        name="rwkv7_mix",
    )(p_r, mu, w0, wdec, a0, wa, wg, k_k, k_a, r_k, lnx_g, lnx_b)


def _swa_stages(p_a, kv_before, has_before, sink_ref, norm_g, o_ref):
    w, hd = WINDOW, HEAD_DIM
    heads = [[kv * ATTN_GROUP + j for j in range(ATTN_GROUP)] for kv in range(ATTN_KV_HEADS)]
    qi = lax.broadcasted_iota(jnp.int32, (w, 2 * w), 0)
    kj = lax.broadcasted_iota(jnp.int32, (w, 2 * w), 1)
    band = (kj > qi) & (kj <= qi + w)
    first = band & ((kj >= w) | has_before)
    stack = lambda m: jnp.concatenate([jnp.where(m, 0.0, -jnp.inf)] * ATTN_GROUP, axis=0)
    neg_first, neg_band = stack(first), stack(band)
    grp_row = lax.broadcasted_iota(jnp.int32, (ATTN_GROUP * w, 1), 0) >> 7
    q = p_a[:, :ATTN_WIDTH] * (hd ** -0.5)
    ones = jnp.ones((2 * w, hd), BF16)

    def keys(col0, b, kv):
        cur = slice(col0 + kv * hd, col0 + (kv + 1) * hd)
        old = slice(col0 - ATTN_WIDTH + kv * hd, col0 - ATTN_WIDTH + (kv + 1) * hd)
        prev = kv_before[:, old] if b == 0 else p_a[(b - 1) * w:b * w, cur]
        return jnp.concatenate([prev, p_a[b * w:(b + 1) * w, cur]], axis=0)

    sink = []
    for kv in range(ATTN_KV_HEADS):
        sk = jnp.full((ATTN_GROUP * w, 1), sink_ref[heads[kv][0]], F32)
        for j in range(1, ATTN_GROUP):
            sk = jnp.where(grp_row == j, sink_ref[heads[kv][j]], sk)
        sink.append(sk)
    for b0 in range(0, p_a.shape[0] // w, SWA_STEP_BLOCKS):
        units = [(b, kv) for b in range(b0, b0 + SWA_STEP_BLOCKS) for kv in range(ATTN_KV_HEADS)]
        s = {}
        for (b, kv) in units:
            qs = jnp.concatenate([q[b * w:(b + 1) * w, h * hd:(h + 1) * hd] for h in heads[kv]], axis=0)
            s[b, kv] = _dot(qs, keys(ATTN_WIDTH, b, kv), _NT)
        yield
        e, sink_e = {}, {}
        for (b, kv) in units:
            sm = s[b, kv] + (neg_first if b == 0 else neg_band)
            m = jnp.maximum(jnp.max(sm, axis=-1, keepdims=True), sink[kv])
            e[b, kv] = jnp.exp(sm - m).astype(BF16)
            sink_e[b, kv] = jnp.exp(sink[kv] - m)
        yield
        o = {}
        for (b, kv) in units:
            pv = _dot(e[b, kv], jnp.concatenate([keys(ATTN_WIDTH + ATTN_KV_WIDTH, b, kv), ones], axis=1))
            denom = pltpu.roll(pv, hd, axis=1) + sink_e[b, kv]
            o[b, kv] = (pv / denom)[:, :hd]
        yield
        for b in range(b0, b0 + SWA_STEP_BLOCKS):
            outs = [None] * ATTN_Q_HEADS
            for kv in range(ATTN_KV_HEADS):
                for j, h in enumerate(heads[kv]):
                    outs[h] = o[b, kv][j * w:(j + 1) * w, :]
            y = jnp.concatenate(outs, axis=-1)
            o_ref[b * w:(b + 1) * w, :] = _rmsnorm(y, norm_g).astype(o_ref.dtype)
        yield


def _out_proj_kernel(yr_ref, ya_ref, yg_ref, wr_ref, wa_ref, wg_ref, x_ref, o_ref):
    acc = (_dot(yr_ref[...], wr_ref[...].astype(BF16)) + _dot(ya_ref[...], wa_ref[...].astype(BF16))
           + _dot(yg_ref[...], wg_ref[...].astype(BF16)))
    o_ref[...] = x_ref[...] + acc


def _out_proj(y_r, y_a, y_g, w, layer, x, *, tm):
    m, d = x.shape
    lhs = lambda n: pl.BlockSpec((tm, n), lambda i: (i, 0))
    rhs = lambda n, blk: pl.BlockSpec((None, n, d), lambda i: (layer, blk, 0), pipeline_mode=pl.Buffered(1))
    return pl.pallas_call(
        _out_proj_kernel,
        grid=(m // tm,),
        in_specs=[lhs(RWKV_WIDTH), lhs(ATTN_WIDTH), lhs(GM_WIDTH),
                  rhs(RWKV_WIDTH, 0), rhs(ATTN_WIDTH, 1), rhs(GM_WIDTH, (RWKV_WIDTH + ATTN_WIDTH) // GM_WIDTH),
                  pl.BlockSpec((tm, d), lambda i: (i, 0))],
        out_specs=pl.BlockSpec((tm, d), lambda i: (i, 0)),
        out_shape=jax.ShapeDtypeStruct((m, d), F32),
        compiler_params=pltpu.CompilerParams(
            dimension_semantics=("parallel",), vmem_limit_bytes=V7X_VMEM_LIMIT),
        name="out_proj_residual",
    )(y_r, y_a, y_g, w, w, w, x)


def _ffn_kernel(x_ref, g_ref, up_ref, down_ref, gf_ref, o_ref, h_ref, *, final_norm):
    @pl.when(pl.program_id(1) == 0)
    def _():
        x = x_ref[...]
        h_ref[...] = _rmsnorm(x, g_ref[...]).astype(BF16)
        o_ref[...] = x

    a = jnp.maximum(_dot(h_ref[...], up_ref[...].astype(BF16)), 0.0)
    o_ref[...] += _dot((a * a).astype(BF16), down_ref[...].astype(BF16))

    if final_norm:
        @pl.when(pl.program_id(1) == pl.num_programs(1) - 1)
        def _():
            o_ref[...] = _rmsnorm(o_ref[...], gf_ref[...])


def _ffn(x, g, w_up, w_down, layer, g_final, *, final_norm, tm, tf):
    m, d = x.shape
    f = w_up.shape[2]
    return pl.pallas_call(
        functools.partial(_ffn_kernel, final_norm=final_norm),
        grid=(m // tm, f // tf),
        in_specs=[
            pl.BlockSpec((tm, d), lambda i, j: (i, 0)),
            pl.BlockSpec((1, d), lambda i, j: (0, 0)),
            pl.BlockSpec((None, d, tf), lambda i, j: (layer, 0, j)),
            pl.BlockSpec((None, tf, d), lambda i, j: (layer, j, 0)),
            pl.BlockSpec((1, d), lambda i, j: (0, 0)),
        ],
        out_specs=pl.BlockSpec((tm, d), lambda i, j: (i, 0)),
        out_shape=jax.ShapeDtypeStruct((m, d), F32),
        scratch_shapes=[pltpu.VMEM((tm, d), BF16)],
        compiler_params=pltpu.CompilerParams(
            dimension_semantics=("parallel", "arbitrary"), vmem_limit_bytes=V7X_VMEM_LIMIT),
        name="relu2_mlp_residual",
    )(x, g, w_up, w_down, g_final)


def kernel(x, ln1_g, w_in, rwkv_mu, rwkv_w0, rwkv_decay_up, rwkv_a0, rwkv_a_up, rwkv_g_up, rwkv_k_k,
           rwkv_k_a, rwkv_r_k, rwkv_lnx_g, rwkv_lnx_b, attn_sinks, attn_norm_g, gm_ln_g, gm_ln_b, gm_ws,
           gm_bs, gm_norm_g, w_out, ln2_g, w_ffn_up, w_ffn_down, lnf_g):
    batch, seq, d = x.shape
    assert (batch, seq, d) == (1, SEQ, D_MODEL)
    xs = x.reshape(seq, d)
    row = lambda a: a.reshape(1, -1)
    zeros_lora = jnp.zeros((LORA_IN - DECAY_LORA, RWKV_WIDTH), F32)
    w_in_b = w_in.astype(BF16)
    for l in range(DEPTH):
        bias_full = jnp.repeat(gm_bs[l].T, GM_HEAD_DIM, axis=1)
        p_r, y_a, y_g = _in_proj_mix(xs, row(ln1_g[l]), w_in_b, l, attn_sinks[l], row(attn_norm_g[l]),
                                     row(gm_ln_g[l]), row(gm_ln_b[l]), gm_ws[l], bias_full,
                                     row(gm_norm_g[l]), tm=512)

        wdec = jnp.concatenate([rwkv_decay_up[l], zeros_lora], axis=0).astype(BF16)
        wa = jnp.concatenate([zeros_lora, rwkv_a_up[l]], axis=0).astype(BF16)
        y_r = _rwkv_mix(p_r, row(rwkv_mu[l]), row(rwkv_w0[l]), wdec, row(rwkv_a0[l]), wa,
                        rwkv_g_up[l].astype(BF16), row(rwkv_k_k[l]), row(rwkv_k_a[l]), row(rwkv_r_k[l]),
                        row(rwkv_lnx_g[l]), row(rwkv_lnx_b[l]))

        xs = _out_proj(y_r, y_a, y_g, w_out, l, xs, tm=512)
        xs = _ffn(xs, row(ln2_g[l]), w_ffn_up, w_ffn_down, l, row(lnf_g), final_norm=(l == DEPTH - 1),
                  tm=1024, tf=512)
    return xs.reshape(batch, seq, d)
```

```python
import functools

import jax
import jax.numpy as jnp
from jax import lax
from jax.experimental import pallas as pl
from jax.experimental.pallas import tpu as pltpu

F32 = jnp.float32
BF16 = jnp.bfloat16

D_MODEL = 2048
SEQ = 8192
DEPTH = 4
HEAD_DIM = 64
RWKV_HEADS = 12
RWKV_WIDTH = RWKV_HEADS * HEAD_DIM
DECAY_LORA = 64
ICLR_LORA = 64
GATE_LORA = 128
LORA_IN = DECAY_LORA + ICLR_LORA
ATTN_Q_HEADS = 12
ATTN_KV_HEADS = 4
ATTN_GROUP = ATTN_Q_HEADS // ATTN_KV_HEADS
ATTN_WIDTH = ATTN_Q_HEADS * HEAD_DIM
ATTN_KV_WIDTH = ATTN_KV_HEADS * HEAD_DIM
WINDOW = 128
SWA_STEP_BLOCKS = 2
GM_HEADS = 4
GM_HEAD_DIM = 128
GM_WIDTH = GM_HEADS * GM_HEAD_DIM
GM_CHUNK = 128
IN_PROJ_CHUNK = 256
RWKV_COLS = 3 * RWKV_WIDTH + DECAY_LORA + ICLR_LORA + GATE_LORA
ATTN_COLS = ATTN_WIDTH + 2 * ATTN_KV_WIDTH
GM_COLS = 2 * GM_WIDTH
NORM_EPS = 1e-6
GN_EPS = 64e-5

RWKV_CHUNK = 64
RWKV_GROUP_HEADS = 4
RWKV_GROUP_WIDTH = RWKV_GROUP_HEADS * HEAD_DIM
RWKV_GROUPS = RWKV_HEADS // RWKV_GROUP_HEADS
RWKV_PAIR_CHUNKS = 2
RWKV_PAIR_ROWS = RWKV_PAIR_CHUNKS * RWKV_CHUNK
RWKV_STEP_PAIRS = 4
RWKV_STEP_ROWS = RWKV_STEP_PAIRS * RWKV_PAIR_ROWS
RWKV_CHUNK_LOG2 = 6
RWKV_INV_BASE_LOG2 = 3

V7X_VMEM_LIMIT = 56 * 1024 * 1024

_NN = (((1,), (0,)), ((), ()))
_NT = (((1,), (1,)), ((), ()))
_TN = (((0,), (0,)), ((), ()))


def _dot(a, b, dims=_NN):
    return lax.dot_general(a, b, dims, preferred_element_type=F32)


def _split_hi_lo(x):
    hi = x.astype(BF16)
    lo = (x - hi.astype(F32)).astype(BF16)
    return hi, lo


def _mm(a, b, dims=_NN):
    return _dot(a.astype(BF16), b.astype(BF16), dims)


def _rmsnorm(x, g):
    return x * lax.rsqrt(jnp.mean(x * x, axis=-1, keepdims=True) + NORM_EPS) * g


def _alternate(*iterators):
    live = list(iterators)
    while live:
        for it in list(live):
            try:
                next(it)
            except StopIteration:
                live.remove(it)


def _gmlp_stages(p, ln_g, ln_b, ws_ref, bias, norm_g, o_ref):
    ch, hd = GM_CHUNK, GM_HEAD_DIM
    z = 0.5 * p * (1.0 + lax.erf(p * (2.0 ** -0.5)))
    yield
    u = z[:, :GM_WIDTH]
    z2 = z[:, GM_WIDTH:]
    mu = jnp.mean(z2, axis=-1, keepdims=True)
    zc = z2 - mu
    var = jnp.mean(zc * zc, axis=-1, keepdims=True)
    z2 = (zc * lax.rsqrt(var + NORM_EPS) * ln_g + ln_b).astype(BF16)
    yield
    ti = lax.broadcasted_iota(jnp.int32, (ch, ch), 0)
    si = lax.broadcasted_iota(jnp.int32, (ch, ch), 1)
    causal = si <= ti
    wh = [jnp.where(causal, ws_ref[h], 0.0).astype(BF16) for h in range(GM_HEADS)]
    for ci in range(p.shape[0] // ch):
        rows = slice(ci * ch, (ci + 1) * ch)
        mixed = [_dot(wh[h], z2[rows, h * hd:(h + 1) * hd]) for h in range(GM_HEADS)]
        y = u[rows, :] * (jnp.concatenate(mixed, axis=-1) + bias)
        o_ref[rows, :] = _rmsnorm(y, norm_g).astype(o_ref.dtype)
        yield


def _in_proj_kernel(sink_ref, x_ref, xb_ref, g_ref, w_ref, lng_ref, lnb_ref, ws_ref, bias_ref, gmg_ref, ang_ref,
                    pr_ref, ya_ref, yg_ref):
    kv0 = RWKV_COLS + ATTN_WIDTH
    h = _rmsnorm(x_ref[...], g_ref[...]).astype(BF16)
    hb = _rmsnorm(xb_ref[...], g_ref[...]).astype(BF16)
    kv_before = _dot(hb, w_ref[:, kv0:kv0 + 2 * ATTN_KV_WIDTH]).astype(BF16)
    p_a = _dot(h, w_ref[:, RWKV_COLS:RWKV_COLS + ATTN_COLS]).astype(BF16)
    p_g = _dot(h, w_ref[:, RWKV_COLS + ATTN_COLS:])

    def rwkv_columns():
        for c0 in range(0, RWKV_COLS, IN_PROJ_CHUNK):
            pr_ref[:, c0:c0 + IN_PROJ_CHUNK] = _dot(h, w_ref[:, c0:c0 + IN_PROJ_CHUNK])
            yield

    _alternate(_swa_stages(p_a, kv_before, pl.program_id(0) > 0, sink_ref, ang_ref[...], ya_ref),
               rwkv_columns(),
               _gmlp_stages(p_g, lng_ref[...], lnb_ref[...], ws_ref, bias_ref[...], gmg_ref[...], yg_ref))


def _in_proj_mix(x, g, w, layer, sinks, attn_norm_g, ln_g, ln_b, ws, bias_full, gm_norm_g, *, tm):
    m, d = x.shape
    n = w.shape[2]
    out = lambda width: pl.BlockSpec((tm, width), lambda i: (i, 0))
    row = lambda width: pl.BlockSpec((1, width), lambda i: (0, 0))
    return pl.pallas_call(
        _in_proj_kernel,
        grid=(m // tm,),
        in_specs=[
            pl.BlockSpec(memory_space=pltpu.SMEM),
            pl.BlockSpec((tm, d), lambda i: (i, 0)),
            pl.BlockSpec((WINDOW, d), lambda i: (jnp.maximum(i * (tm // WINDOW) - 1, 0), 0)),
            row(d),
            pl.BlockSpec((None, d, n), lambda i: (layer, 0, 0), pipeline_mode=pl.Buffered(1)),
            row(GM_WIDTH), row(GM_WIDTH),
            pl.BlockSpec((GM_HEADS, GM_CHUNK, GM_CHUNK), lambda i: (0, 0, 0)),
            pl.BlockSpec((GM_CHUNK, GM_WIDTH), lambda i: (0, 0)),
            row(GM_WIDTH), row(ATTN_WIDTH),
        ],
        out_specs=[out(RWKV_COLS), out(ATTN_WIDTH), out(GM_WIDTH)],
        out_shape=[jax.ShapeDtypeStruct((m, RWKV_COLS), F32), jax.ShapeDtypeStruct((m, ATTN_WIDTH), BF16),
                   jax.ShapeDtypeStruct((m, GM_WIDTH), BF16)],
        compiler_params=pltpu.CompilerParams(
            dimension_semantics=("parallel",), vmem_limit_bytes=V7X_VMEM_LIMIT),
        name="norm_in_proj_mix",
    )(sinks, x, x, g, w, ln_g, ln_b, ws, bias_full, gm_norm_g, attn_norm_g)


def _rwkv_kernel(p_ref, mu_ref, w0_ref, wdec_ref, a0_ref, wa_ref, wg_ref, kk_ref, ka_ref, rk_ref,
                 lng_ref, lnb_ref, o_ref, prev_ref, s_ref):
    c, gw, prows = RWKV_CHUNK, RWKV_GROUP_WIDTH, RWKV_PAIR_ROWS
    w = RWKV_WIDTH
    groups = range(RWKV_GROUPS)
    chunks = range(RWKV_PAIR_CHUNKS)
    units = [(ci, g) for ci in chunks for g in groups]

    @pl.when(pl.program_id(0) == 0)
    def _():
        prev_ref[...] = jnp.zeros_like(prev_ref)
        s_ref[...] = jnp.zeros_like(s_ref)

    rowh = lax.broadcasted_iota(jnp.int32, (gw, gw), 0) >> 6
    colh = lax.broadcasted_iota(jnp.int32, (gw, gw), 1) >> 6
    bdmask = rowh == colh
    head_ones = bdmask.astype(BF16)
    ti = lax.broadcasted_iota(jnp.int32, (prows, prows), 0)
    si = lax.broadcasted_iota(jnp.int32, (prows, prows), 1)
    tril_ones = ((si <= ti) & ((si >> 6) == (ti >> 6))).astype(BF16)
    t2 = lax.broadcasted_iota(jnp.int32, (c, gw), 0)
    s2 = lax.broadcasted_iota(jnp.int32, (c, gw), 1) & (c - 1)
    strict = s2 < t2
    incl = s2 <= t2
    eye4 = (s2 == t2).astype(F32)
    lane = lax.broadcasted_iota(jnp.int32, (c, 2 * HEAD_DIM), 1)
    even_head = (lane < HEAD_DIM).astype(BF16)
    odd_head = (lane >= HEAD_DIM).astype(BF16)
    zero_slab = jnp.zeros((c, 2 * HEAD_DIM), BF16)
    first_row = lax.broadcasted_iota(jnp.int32, (8, RWKV_COLS), 0) == 0

    def head_sum(x):
        return _dot(x.astype(BF16), head_ones)

    def bd(x):
        x = x.astype(BF16)
        lo, hi = x[:, :2 * HEAD_DIM], x[:, 2 * HEAD_DIM:]
        return jnp.concatenate([
            jnp.concatenate([lo * even_head, zero_slab], axis=1),
            jnp.concatenate([lo * odd_head, zero_slab], axis=1),
            jnp.concatenate([zero_slab, hi * even_head], axis=1),
            jnp.concatenate([zero_slab, hi * odd_head], axis=1)], axis=0)

    def cols(x, g):
        return x[:, g * gw:(g + 1) * gw]

    def chunk(x, ci):
        return x[ci * c:(ci + 1) * c, :]

    state = [s_ref[g] for g in groups]
    ops = {}

    def operands(pi):
        r0 = pi * prows
        p = p_ref[r0:r0 + prows, :]
        before = prev_ref[...] if pi == 0 else p_ref[r0 - 1:r0, :]
        rolled = pltpu.roll(p, 1, axis=0)
        shifted = jnp.concatenate([jnp.where(first_row, before, rolled[:8]), rolled[8:]], axis=0)
        p = p + (shifted - p) * mu_ref[...]
        yield
        x_lora = p[:, 3 * w:3 * w + LORA_IN]
        x_gate = p[:, 3 * w + LORA_IN:]
        dec = -(jax.nn.softplus(-(w0_ref[...] + _mm(jnp.tanh(x_lora), wdec_ref[...])))) - 0.5
        logd_all = -jnp.exp(dec)
        alpha_all = jax.nn.sigmoid(a0_ref[...] + _mm(x_lora, wa_ref[...]))
        o = ops[pi] = {"gate": _mm(jax.nn.sigmoid(x_gate), wg_ref[...])}
        for name in ("at", "rt", "bt", "kt", "v", "wend", "dend", "bonus"):
            o[name] = {}
        yield
        for g in groups:
            r, k, v = cols(p[:, 0:w], g), cols(p[:, w:2 * w], g), cols(p[:, 2 * w:3 * w], g)
            alpha, logd = cols(alpha_all, g), cols(logd_all, g)
            kk = k * cols(kk_ref[...], g)
            kk_sq = head_sum(kk * kk)
            dh, dl = _split_hi_lo(logd)
            cum = _dot(tril_ones, dh) + _dot(tril_ones, dl)
            k2 = k * (1.0 + (alpha - 1.0) * cols(ka_ref[...], g))
            o["bonus"][g] = head_sum(r * k2 * cols(rk_ref[...], g)) * v
            o["v"][g] = v.astype(BF16)
            yield
            kkn = kk / jnp.maximum(jnp.sqrt(kk_sq), 1e-12)
            b = kkn * alpha
            inv = jnp.exp(-cum)
            o["at"][g] = (-kkn * jnp.exp(cum - logd)).astype(BF16)
            o["rt"][g] = (r * jnp.exp(cum)).astype(BF16)
            yield
            o["bt"][g] = (b * inv).astype(BF16)
            o["kt"][g] = (k2 * inv).astype(BF16)
            yield
            for ci in chunks:
                cum_end = chunk(cum, ci)[c - 1:c, :]
                to_end = jnp.exp(cum_end - chunk(cum, ci))
                o["wend"][ci, g] = jnp.concatenate(
                    [(chunk(b, ci) * to_end).astype(BF16), (chunk(k2, ci) * to_end).astype(BF16)], axis=0)
                o["dend"][ci, g] = jnp.exp(cum_end)
            yield

    def head(pi):
        o = ops[pi]
        x, ab, ak, vq = {}, {}, {}, {}
        for (ci, g) in units:
            x[ci, g] = jnp.concatenate([chunk(o["at"][g], ci), chunk(o["rt"][g], ci)], axis=0)
            vq[ci, g] = chunk(o["v"][g], ci)
            ab[ci, g] = _dot(x[ci, g], bd(chunk(o["bt"][g], ci)), _NT)
            ak[ci, g] = _dot(x[ci, g], bd(chunk(o["kt"][g], ci)), _NT)
        yield
        a_ab = {u_: jnp.where(strict, ab[u_][:c], 0.0) for u_ in units}
        a_rb = {u_: jnp.where(incl, ab[u_][c:], 0.0) for u_ in units}
        a_ak = {u_: jnp.where(strict, ak[u_][:c], 0.0) for u_ in units}
        a_rk = {u_: jnp.where(incl, ak[u_][c:], 0.0) for u_ in units}

        blk_t, blk_s = t2 >> 3, s2 >> 3
        a_dg = {u_: jnp.where(blk_t == blk_s, a_ab[u_], 0.0) for u_ in units}
        t_inv = {u_: eye4 + a_dg[u_] for u_ in units}
        pw = {u_: _mm(a_dg[u_], bd(a_dg[u_])) for u_ in units}
        av = {u_: _mm(jnp.concatenate([a_ak[u_], a_rk[u_]], axis=0), bd(vq[u_])) for u_ in units}
        yield
        tp = {u_: _mm(jnp.concatenate([t_inv[u_], pw[u_]], axis=0), bd(pw[u_])) for u_ in units}
        yield
        t_inv = {u_: t_inv[u_] + tp[u_][:c] for u_ in units}
        t_inv = {u_: t_inv[u_] + _mm(t_inv[u_], bd(tp[u_][c:])) for u_ in units}
        yield
        for shift in range(RWKV_INV_BASE_LOG2, RWKV_CHUNK_LOG2):
            couple = ((t2 >> (shift + 1)) == (s2 >> (shift + 1))) & ((t2 >> shift) != (s2 >> shift))
            de = {u_: _mm(t_inv[u_], bd(jnp.where(couple, a_ab[u_], 0.0))) for u_ in units}
            yield
            t_inv = {u_: t_inv[u_] + _mm(de[u_], bd(t_inv[u_])) for u_ in units}
            yield
        o.update(x=x, vq=vq, a_rb=a_rb, av=av, t_inv=t_inv)

    def tail(pi):
        o = ops[pi]
        x, vq, a_rb, av, t_inv = o["x"], o["vq"], o["a_rb"], o["av"], o["t_inv"]
        y = {}
        for ci in chunks:
            xs = [_mm(x[ci, g], state[g], _NT) for g in groups]
            yield
            u = [_mm(t_inv[ci, g], bd(xs[g][:c] + av[ci, g][:c])) for g in groups]
            yield
            for g in groups:
                y[ci, g] = xs[g][c:] + _mm(a_rb[ci, g], bd(u[g])) + av[ci, g][c:]
            for g in groups:
                upd = _dot(jnp.concatenate([u[g].astype(BF16), vq[ci, g]], axis=0), o["wend"][ci, g], _TN)
                state[g] = state[g] * o["dend"][ci, g] + jnp.where(bdmask, upd, 0.0)
            yield

        yf = [jnp.concatenate([y[ci, g] for ci in chunks], axis=0) for g in groups]
        mean = [head_sum(yf[g]) * (1.0 / HEAD_DIM) for g in groups]
        yield
        yc = [yf[g] - mean[g] for g in groups]
        var = [head_sum(yc[g] * yc[g]) * (1.0 / HEAD_DIM) for g in groups]
        yield
        r0 = pi * prows
        for g in groups:
            yn = yc[g] * lax.rsqrt(var[g] + GN_EPS) * cols(lng_ref[...], g) + cols(lnb_ref[...], g)
            o_ref[r0:r0 + prows, g * gw:(g + 1) * gw] = (
                (yn + o["bonus"][g]) * cols(o["gate"], g)).astype(o_ref.dtype)
        del ops[pi]

    npairs = RWKV_STEP_PAIRS
    _alternate(operands(0))
    _alternate(head(0), *([operands(1)] if npairs > 1 else []))
    for pi in range(npairs):
        _alternate(*([head(pi + 1)] if pi + 1 < npairs else []), tail(pi),
                   *([operands(pi + 2)] if pi + 2 < npairs else []))
    prev_ref[...] = p_ref[RWKV_STEP_ROWS - 1:RWKV_STEP_ROWS, :]
    for g in groups:
        s_ref[g] = state[g]


def _rwkv_mix(p_r, mu, w0, wdec, a0, wa, wg, k_k, k_a, r_k, lnx_g, lnx_b):
    s = p_r.shape[0]
    rows = RWKV_STEP_ROWS
    row = lambda n: pl.BlockSpec((1, n), lambda i: (0, 0))
    full = lambda a: pl.BlockSpec(a.shape, lambda i: (0, 0))
    return pl.pallas_call(
        _rwkv_kernel,
        grid=(s // rows,),
        in_specs=[
            pl.BlockSpec((rows, RWKV_COLS), lambda i: (i, 0)),
            row(RWKV_COLS), row(RWKV_WIDTH), full(wdec), row(RWKV_WIDTH), full(wa), full(wg),
            row(RWKV_WIDTH), row(RWKV_WIDTH), row(RWKV_WIDTH), row(RWKV_WIDTH), row(RWKV_WIDTH),
        ],
        out_specs=pl.BlockSpec((rows, RWKV_WIDTH), lambda i: (i, 0)),
        out_shape=jax.ShapeDtypeStruct((s, RWKV_WIDTH), BF16),
        scratch_shapes=[
            pltpu.VMEM((1, RWKV_COLS), F32),
            pltpu.VMEM((RWKV_GROUPS, RWKV_GROUP_WIDTH, RWKV_GROUP_WIDTH), F32),
        ],
        compiler_params=pltpu.CompilerParams(
            dimension_semantics=("arbitrary",), vmem_limit_bytes=V7X_VMEM_LIMIT),
        name="rwkv7_mix",
    )(p_r, mu, w0, wdec, a0, wa, wg, k_k, k_a, r_k, lnx_g, lnx_b)


def _swa_stages(p_a, kv_before, has_before, sink_ref, norm_g, o_ref):
    w, hd = WINDOW, HEAD_DIM
    heads = [[kv * ATTN_GROUP + j for j in range(ATTN_GROUP)] for kv in range(ATTN_KV_HEADS)]
    qi = lax.broadcasted_iota(jnp.int32, (w, 2 * w), 0)
    kj = lax.broadcasted_iota(jnp.int32, (w, 2 * w), 1)
    band = (kj > qi) & (kj <= qi + w)
    first = band & ((kj >= w) | has_before)
    stack = lambda m: jnp.concatenate([jnp.where(m, 0.0, -jnp.inf)] * ATTN_GROUP, axis=0)
    neg_first, neg_band = stack(first), stack(band)
    grp_row = lax.broadcasted_iota(jnp.int32, (ATTN_GROUP * w, 1), 0) >> 7
    q = p_a[:, :ATTN_WIDTH] * (hd ** -0.5)
    ones = jnp.ones((2 * w, hd), BF16)

    def keys(col0, b, kv):
        cur = slice(col0 + kv * hd, col0 + (kv + 1) * hd)
        old = slice(col0 - ATTN_WIDTH + kv * hd, col0 - ATTN_WIDTH + (kv + 1) * hd)
        prev = kv_before[:, old] if b == 0 else p_a[(b - 1) * w:b * w, cur]
        return jnp.concatenate([prev, p_a[b * w:(b + 1) * w, cur]], axis=0)

    sink = []
    for kv in range(ATTN_KV_HEADS):
        sk = jnp.full((ATTN_GROUP * w, 1), sink_ref[heads[kv][0]], F32)
        for j in range(1, ATTN_GROUP):
            sk = jnp.where(grp_row == j, sink_ref[heads[kv][j]], sk)
        sink.append(sk)
    for b0 in range(0, p_a.shape[0] // w, SWA_STEP_BLOCKS):
        units = [(b, kv) for b in range(b0, b0 + SWA_STEP_BLOCKS) for kv in range(ATTN_KV_HEADS)]
        s = {}
        for (b, kv) in units:
            qs = jnp.concatenate([q[b * w:(b + 1) * w, h * hd:(h + 1) * hd] for h in heads[kv]], axis=0)
            s[b, kv] = _dot(qs, keys(ATTN_WIDTH, b, kv), _NT)
        yield
        e, sink_e = {}, {}
        for (b, kv) in units:
            sm = s[b, kv] + (neg_first if b == 0 else neg_band)
            m = jnp.maximum(jnp.max(sm, axis=-1, keepdims=True), sink[kv])
            e[b, kv] = jnp.exp(sm - m).astype(BF16)
            sink_e[b, kv] = jnp.exp(sink[kv] - m)
        yield
        o = {}
        for (b, kv) in units:
            pv = _dot(e[b, kv], jnp.concatenate([keys(ATTN_WIDTH + ATTN_KV_WIDTH, b, kv), ones], axis=1))
            denom = pltpu.roll(pv, hd, axis=1) + sink_e[b, kv]
            o[b, kv] = (pv / denom)[:, :hd]
        yield
        for b in range(b0, b0 + SWA_STEP_BLOCKS):
            outs = [None] * ATTN_Q_HEADS
            for kv in range(ATTN_KV_HEADS):
                for j, h in enumerate(heads[kv]):
                    outs[h] = o[b, kv][j * w:(j + 1) * w, :]
            y = jnp.concatenate(outs, axis=-1)
            o_ref[b * w:(b + 1) * w, :] = _rmsnorm(y, norm_g).astype(o_ref.dtype)
        yield


def _out_proj_kernel(yr_ref, ya_ref, yg_ref, wr_ref, wa_ref, wg_ref, x_ref, o_ref):
    acc = (_dot(yr_ref[...], wr_ref[...].astype(BF16)) + _dot(ya_ref[...], wa_ref[...].astype(BF16))
           + _dot(yg_ref[...], wg_ref[...].astype(BF16)))
    o_ref[...] = x_ref[...] + acc


def _out_proj(y_r, y_a, y_g, w, layer, x, *, tm):
    m, d = x.shape
    lhs = lambda n: pl.BlockSpec((tm, n), lambda i: (i, 0))
    rhs = lambda n, blk: pl.BlockSpec((None, n, d), lambda i: (layer, blk, 0), pipeline_mode=pl.Buffered(1))
    return pl.pallas_call(
        _out_proj_kernel,
        grid=(m // tm,),
        in_specs=[lhs(RWKV_WIDTH), lhs(ATTN_WIDTH), lhs(GM_WIDTH),
                  rhs(RWKV_WIDTH, 0), rhs(ATTN_WIDTH, 1), rhs(GM_WIDTH, (RWKV_WIDTH + ATTN_WIDTH) // GM_WIDTH),
                  pl.BlockSpec((tm, d), lambda i: (i, 0))],
        out_specs=pl.BlockSpec((tm, d), lambda i: (i, 0)),
        out_shape=jax.ShapeDtypeStruct((m, d), F32),
        compiler_params=pltpu.CompilerParams(
            dimension_semantics=("parallel",), vmem_limit_bytes=V7X_VMEM_LIMIT),
        name="out_proj_residual",
    )(y_r, y_a, y_g, w, w, w, x)


def _ffn_kernel(x_ref, g_ref, up_ref, down_ref, gf_ref, o_ref, h_ref, *, final_norm):
    @pl.when(pl.program_id(1) == 0)
    def _():
        x = x_ref[...]
        h_ref[...] = _rmsnorm(x, g_ref[...]).astype(BF16)
        o_ref[...] = x

    a = jnp.maximum(_dot(h_ref[...], up_ref[...].astype(BF16)), 0.0)
    o_ref[...] += _dot((a * a).astype(BF16), down_ref[...].astype(BF16))

    if final_norm:
        @pl.when(pl.program_id(1) == pl.num_programs(1) - 1)
        def _():
            o_ref[...] = _rmsnorm(o_ref[...], gf_ref[...])


def _ffn(x, g, w_up, w_down, layer, g_final, *, final_norm, tm, tf):
    m, d = x.shape
    f = w_up.shape[2]
    return pl.pallas_call(
        functools.partial(_ffn_kernel, final_norm=final_norm),
        grid=(m // tm, f // tf),
        in_specs=[
            pl.BlockSpec((tm, d), lambda i, j: (i, 0)),
            pl.BlockSpec((1, d), lambda i, j: (0, 0)),
            pl.BlockSpec((None, d, tf), lambda i, j: (layer, 0, j)),
            pl.BlockSpec((None, tf, d), lambda i, j: (layer, j, 0)),
            pl.BlockSpec((1, d), lambda i, j: (0, 0)),
        ],
        out_specs=pl.BlockSpec((tm, d), lambda i, j: (i, 0)),
        out_shape=jax.ShapeDtypeStruct((m, d), F32),
        scratch_shapes=[pltpu.VMEM((tm, d), BF16)],
        compiler_params=pltpu.CompilerParams(
            dimension_semantics=("parallel", "arbitrary"), vmem_limit_bytes=V7X_VMEM_LIMIT),
        name="relu2_mlp_residual",
    )(x, g, w_up, w_down, g_final)


def kernel(x, ln1_g, w_in, rwkv_mu, rwkv_w0, rwkv_decay_up, rwkv_a0, rwkv_a_up, rwkv_g_up, rwkv_k_k,
           rwkv_k_a, rwkv_r_k, rwkv_lnx_g, rwkv_lnx_b, attn_sinks, attn_norm_g, gm_ln_g, gm_ln_b, gm_ws,
           gm_bs, gm_norm_g, w_out, ln2_g, w_ffn_up, w_ffn_down, lnf_g):
    batch, seq, d = x.shape
    assert (batch, seq, d) == (1, SEQ, D_MODEL)
    xs = x.reshape(seq, d)
    row = lambda a: a.reshape(1, -1)
    zeros_lora = jnp.zeros((LORA_IN - DECAY_LORA, RWKV_WIDTH), F32)
    w_in_b = w_in.astype(BF16)
    for l in range(DEPTH):
        bias_full = jnp.repeat(gm_bs[l].T, GM_HEAD_DIM, axis=1)
        p_r, y_a, y_g = _in_proj_mix(xs, row(ln1_g[l]), w_in_b, l, attn_sinks[l], row(attn_norm_g[l]),
                                     row(gm_ln_g[l]), row(gm_ln_b[l]), gm_ws[l], bias_full,
                                     row(gm_norm_g[l]), tm=512)

        wdec = jnp.concatenate([rwkv_decay_up[l], zeros_lora], axis=0).astype(BF16)
        wa = jnp.concatenate([zeros_lora, rwkv_a_up[l]], axis=0).astype(BF16)
        y_r = _rwkv_mix(p_r, row(rwkv_mu[l]), row(rwkv_w0[l]), wdec, row(rwkv_a0[l]), wa,
                        rwkv_g_up[l].astype(BF16), row(rwkv_k_k[l]), row(rwkv_k_a[l]), row(rwkv_r_k[l]),
                        row(rwkv_lnx_g[l]), row(rwkv_lnx_b[l]))

        xs = _out_proj(y_r, y_a, y_g, w_out, l, xs, tm=512)
        xs = _ffn(xs, row(ln2_g[l]), w_ffn_up, w_ffn_down, l, row(lnf_g), final_norm=(l == DEPTH - 1),
                  tm=1024, tf=512)
    return xs.reshape(batch, seq, d)
```
